```python
import math, functools
import jax, jax.numpy as jnp
from jax import lax
import numpy as np

D_MODEL = 4096
BATCH = 4
SEQ = 2048
DEPTH = 2
DEC_BATCH = 8
DEC_SEQ = 4
PAST_LEN = 16384
PAGE_SIZE = 128

ATT_WIDTH = D_MODEL // 2
D_HEAD = 64
H_ATT = ATT_WIDTH // (2 * D_HEAD)
SSM_WIDTH = D_MODEL - ATT_WIDTH
SSM_HEAD_DIM = 64
H_SSM = SSM_WIDTH // SSM_HEAD_DIM
SSM_GROUPS = 8
HEADS_PER_GROUP = H_SSM // SSM_GROUPS
SSM_STATE = 128
CONV_K = 4
CONV_DIM = SSM_WIDTH + 2 * SSM_GROUPS * SSM_STATE
D_FF = 4 * D_MODEL
SSD_CHUNK = 128
Q_BLOCK = 128
IN_COLS = 3 * ATT_WIDTH + SSM_WIDTH + CONV_DIM + H_SSM
NORM_EPS = 1e-6
NEG_BIG = -1e30

kernel_name = 'hymba_diffattn_ssd_step'


def rmsnorm(x, g):
    xf = x.astype(jnp.float32)
    xf = xf * lax.rsqrt(jnp.mean(xf * xf, axis=-1, keepdims=True) + NORM_EPS)
    return (xf * g.astype(jnp.float32)).astype(x.dtype)


def diff_weights(s, mask, lam):
    p = jax.nn.softmax(jnp.where(mask, s, NEG_BIG), axis=-1)
    return p[0] - lam * p[1]


def diff_attn_prompt(q, k, v, lam):
    b, L = q.shape[:2]
    nb = L // Q_BLOCK
    scale = D_HEAD ** -0.5
    qb = q.reshape(b, nb, Q_BLOCK, H_ATT, 2, D_HEAD).swapaxes(0, 1)
    k2 = k.reshape(b, L, H_ATT, 2, D_HEAD)
    kpos = jnp.arange(L)

    def block(args):
        q_blk, start = args
        s = jnp.einsum('bqhid,bkhid->ibhqk', q_blk, k2).astype(jnp.float32) * scale
        qpos = start + jnp.arange(Q_BLOCK)
        w = diff_weights(s, qpos[:, None] >= kpos[None, :], lam)
        return jnp.einsum('bhqk,bkhe->bqhe', w.astype(v.dtype), v)

    out = lax.map(block, (qb, jnp.arange(nb) * Q_BLOCK))
    return out.swapaxes(0, 1).reshape(b, L, H_ATT, 2 * D_HEAD)


def diff_attn_sample(q, k, v, lam, k_past, v_past):
    b, T = q.shape[:2]
    P = k_past.shape[1]
    scale = D_HEAD ** -0.5
    q2 = q.reshape(b, T, H_ATT, 2, D_HEAD)
    s_past = jnp.einsum('bqhid,bkhid->ibhqk', q2, k_past.reshape(b, P, H_ATT, 2, D_HEAD))
    s_new = jnp.einsum('bqhid,bkhid->ibhqk', q2, k.reshape(b, T, H_ATT, 2, D_HEAD))
    s = jnp.concatenate([s_past, s_new], axis=-1).astype(jnp.float32) * scale
    mask = jnp.concatenate([jnp.ones((T, P), bool), jnp.tril(jnp.ones((T, T), bool))], axis=-1)
    w = diff_weights(s, mask, lam).astype(v.dtype)
    return (jnp.einsum('bhqk,bkhe->bqhe', w[..., :P], v_past)
            + jnp.einsum('bhqk,bkhe->bqhe', w[..., P:], v))


def ssd_chunked(X, A, B, C, h0):
    b, L = X.shape[:2]
    T = SSD_CHUNK if L % SSD_CHUNK == 0 else L
    nc = L // T
    X = X.reshape(b, nc, T, SSM_GROUPS, HEADS_PER_GROUP, SSM_HEAD_DIM)
    A = A.reshape(b, nc, T, SSM_GROUPS, HEADS_PER_GROUP)
    B = B.reshape(b, nc, T, SSM_GROUPS, SSM_STATE)
    C = C.reshape(b, nc, T, SSM_GROUPS, SSM_STATE)
    A_cs = jnp.cumsum(A, axis=2)
    seg = A_cs[:, :, :, None] - A_cs[:, :, None, :]
    causal = jnp.tril(jnp.ones((T, T), bool))[:, :, None, None]
    Lmat = jnp.exp(jnp.where(causal, seg, -jnp.inf))
    CB = jnp.einsum('bctgn,bcsgn->bctsg', C, B)
    y_diag = jnp.einsum('bctsg,bctsgr,bcsgrp->bctgrp', CB, Lmat, X)
    decay_to_end = jnp.exp(A_cs[:, :, -1:] - A_cs)
    chunk_states = jnp.einsum('bctgn,bctgr,bctgrp->bcgrpn', B, decay_to_end, X)
    chunk_decay = jnp.exp(A_cs[:, :, -1])

    def step(h, inp):
        s_c, d_c = inp
        return h * d_c[..., None, None] + s_c, h

    h_final, h_in = lax.scan(step, h0, (chunk_states.swapaxes(0, 1), chunk_decay.swapaxes(0, 1)))
    h_in = h_in.swapaxes(0, 1)
    y_off = jnp.einsum('bctgn,bcgrpn,bctgr->bctgrp', C, h_in, jnp.exp(A_cs))
    y = (y_diag + y_off).reshape(b, L, SSM_GROUPS, HEADS_PER_GROUP, SSM_HEAD_DIM)
    return y, h_final


def ssd_mixer(z, xbc, dt_raw, conv_prev, ssm_prev, conv_w, conv_b, dt_bias, a_log, d_skip, ssm_norm_g):
    b, L = xbc.shape[:2]
    xpad = jnp.concatenate([conv_prev.astype(xbc.dtype), xbc], axis=1)
    new_conv = xpad[:, -(CONV_K - 1):]
    acc = conv_b
    for j in range(CONV_K):
        acc = acc + xpad[:, j:j + L] * conv_w[j]
    xbc_c = jax.nn.silu(acc.astype(jnp.float32))
    gn = SSM_GROUPS * SSM_STATE
    xs = xbc_c[..., :SSM_WIDTH].reshape(b, L, SSM_GROUPS, HEADS_PER_GROUP, SSM_HEAD_DIM)
    Bm = xbc_c[..., SSM_WIDTH:SSM_WIDTH + gn].reshape(b, L, SSM_GROUPS, SSM_STATE)
    Cm = xbc_c[..., SSM_WIDTH + gn:].reshape(b, L, SSM_GROUPS, SSM_STATE)
    dt = jax.nn.softplus(dt_raw.astype(jnp.float32) + dt_bias.astype(jnp.float32))
    dt = dt.reshape(b, L, SSM_GROUPS, HEADS_PER_GROUP)
    A = -jnp.exp(a_log.astype(jnp.float32)).reshape(SSM_GROUPS, HEADS_PER_GROUP)
    h0 = ssm_prev.astype(jnp.float32).reshape(b, SSM_GROUPS, HEADS_PER_GROUP, SSM_HEAD_DIM, SSM_STATE)
    y, h_final = ssd_chunked(xs * dt[..., None], A * dt, Bm, Cm, h0)
    y = y + d_skip.astype(jnp.float32).reshape(SSM_GROUPS, HEADS_PER_GROUP, 1) * xs
    y = y.reshape(b, L, SSM_WIDTH) * jax.nn.silu(z.astype(jnp.float32))
    y = rmsnorm(y.reshape(b, L, SSM_GROUPS, SSM_WIDTH // SSM_GROUPS),
                ssm_norm_g.reshape(SSM_GROUPS, SSM_WIDTH // SSM_GROUPS))
    return (y.reshape(b, L, SSM_WIDTH).astype(z.dtype), new_conv,
            h_final.reshape(b, H_SSM, SSM_HEAD_DIM, SSM_STATE))


def hybrid_layer(x, attn_fn, conv_prev, ssm_prev, lam_init, attn_norm_g, w_in, lambda_qk, subln_g,
                 conv_w, conv_b, dt_bias, a_log, d_skip, ssm_norm_g, w_out, mlp_norm_g, w_up, w_down):
    b, L, _ = x.shape
    u = rmsnorm(x, attn_norm_g)
    proj = jnp.einsum('bld,dc->blc', u, w_in)
    idx = np.cumsum([ATT_WIDTH, ATT_WIDTH, ATT_WIDTH, SSM_WIDTH, CONV_DIM]).tolist()
    q, k, v, z, xbc, dt_raw = jnp.split(proj, idx, axis=-1)
    q = q.reshape(b, L, H_ATT, 2 * D_HEAD)
    k = k.reshape(b, L, H_ATT, 2 * D_HEAD)
    v = v.reshape(b, L, H_ATT, 2 * D_HEAD)
    lq = lambda_qk.astype(jnp.float32)
    lam = jnp.exp(jnp.sum(lq[0] * lq[1])) - jnp.exp(jnp.sum(lq[2] * lq[3])) + lam_init
    o = attn_fn(q, k, v, lam)
    o = (rmsnorm(o, subln_g) * (1.0 - lam_init)).reshape(b, L, ATT_WIDTH)
    y_ssm, new_conv, new_ssm = ssd_mixer(z, xbc, dt_raw, conv_prev, ssm_prev, conv_w, conv_b,
                                         dt_bias, a_log, d_skip, ssm_norm_g)
    mix = jnp.concatenate([o, y_ssm.astype(o.dtype)], axis=-1)
    h = x + jnp.einsum('blm,md->bld', mix, w_out)
    m = rmsnorm(h, mlp_norm_g)
    a = jnp.square(jax.nn.relu(jnp.einsum('bld,df->blf', m, w_up)))
    return h + jnp.einsum('blf,fd->bld', a, w_down), k, v, new_conv, new_ssm


def setup_inputs(seed: int = 0) -> dict:
    key = jax.random.key(seed)
    ks = jax.random.split(key, 24)
    f32 = jnp.float32
    n_pages = PAST_LEN // PAGE_SIZE
    n_used = DEC_BATCH * n_pages
    n_pool = n_used + max(1, n_used // 4)

    def nrm(k, shape, s):
        return jax.random.normal(k, shape, f32) * s

    dt = jnp.exp(jax.random.uniform(ks[13], (DEPTH, H_SSM), f32, math.log(1e-3), math.log(1e-1)))
    return {
        'x_prompt': nrm(ks[0], (BATCH, SEQ, D_MODEL), 1.0),
        'x_sample': nrm(ks[1], (DEC_BATCH, DEC_SEQ, D_MODEL), 1.0),
        'cache_k': nrm(ks[2], (DEPTH, n_pool, PAGE_SIZE, H_ATT, 2 * D_HEAD), 1.0),
        'cache_v': nrm(ks[3], (DEPTH, n_pool, PAGE_SIZE, H_ATT, 2 * D_HEAD), 1.0),
        'state_conv': nrm(ks[4], (DEPTH, DEC_BATCH, CONV_K - 1, CONV_DIM), 1.0),
        'state_ssm': nrm(ks[5], (DEPTH, DEC_BATCH, H_SSM, SSM_HEAD_DIM, SSM_STATE), 0.5),
        'page_table': jax.random.permutation(ks[6], n_pool)[:n_used].reshape(DEC_BATCH, n_pages).astype(jnp.int32),
        'attn_norm_g': 1.0 + nrm(ks[7], (DEPTH, D_MODEL), 0.02),
        'w_in': nrm(ks[8], (DEPTH, D_MODEL, IN_COLS), D_MODEL ** -0.5),
        'lambda_qk': nrm(ks[9], (DEPTH, 4, D_HEAD), 0.1),
        'subln_g': 1.0 + nrm(ks[10], (DEPTH, 2 * D_HEAD), 0.02),
        'conv_w': nrm(ks[11], (DEPTH, CONV_K, CONV_DIM), CONV_K ** -0.5),
        'conv_b': nrm(ks[12], (DEPTH, CONV_DIM), 0.01),
        'dt_bias': dt + jnp.log(-jnp.expm1(-dt)),
        'a_log': jnp.log(jax.random.uniform(ks[14], (DEPTH, H_SSM), f32, 1.0, 16.0)),
        'd_skip': 1.0 + nrm(ks[15], (DEPTH, H_SSM), 0.1),
        'ssm_norm_g': 1.0 + nrm(ks[16], (DEPTH, SSM_WIDTH), 0.02),
        'w_out': nrm(ks[17], (DEPTH, ATT_WIDTH + SSM_WIDTH, D_MODEL), (ATT_WIDTH + SSM_WIDTH) ** -0.5),
        'mlp_norm_g': 1.0 + nrm(ks[18], (DEPTH, D_MODEL), 0.02),
        'w_up': nrm(ks[19], (DEPTH, D_MODEL, D_FF), D_MODEL ** -0.5),
        'w_down': nrm(ks[20], (DEPTH, D_FF, D_MODEL), D_FF ** -0.5),
        'final_norm_g': 1.0 + nrm(ks[21], (D_MODEL,), 0.02),
    }


def reference(x_prompt, x_sample, cache_k, cache_v, state_conv, state_ssm, page_table, attn_norm_g, w_in,
              lambda_qk, subln_g, conv_w, conv_b, dt_bias, a_log, d_skip, ssm_norm_g, w_out, mlp_norm_g,
              w_up, w_down, final_norm_g):
    bp = x_prompt.shape[0]
    db = x_sample.shape[0]
    past = page_table.shape[1] * cache_k.shape[2]
    hp, hs = x_prompt, x_sample
    kp_l, vp_l, cp_l, sp_l, ks_l, vs_l, cs_l, ss_l = [], [], [], [], [], [], [], []
    for l in range(DEPTH):
        lam_init = 0.8 - 0.6 * math.exp(-0.3 * l)
        lw = (attn_norm_g[l], w_in[l], lambda_qk[l], subln_g[l], conv_w[l], conv_b[l], dt_bias[l],
              a_log[l], d_skip[l], ssm_norm_g[l], w_out[l], mlp_norm_g[l], w_up[l], w_down[l])
        conv0 = jnp.zeros((bp, CONV_K - 1, CONV_DIM), x_prompt.dtype)
        ssm0 = jnp.zeros((bp, H_SSM, SSM_HEAD_DIM, SSM_STATE), jnp.float32)
        hp, kp, vp, cp, sp = hybrid_layer(hp, diff_attn_prompt, conv0, ssm0, lam_init, *lw)
        k_past = cache_k[l, page_table].reshape(db, past, H_ATT, 2 * D_HEAD)
        v_past = cache_v[l, page_table].reshape(db, past, H_ATT, 2 * D_HEAD)
        attn_s = functools.partial(diff_attn_sample, k_past=k_past, v_past=v_past)
        hs, kn, vn, cn, sn = hybrid_layer(hs, attn_s, state_conv[l], state_ssm[l], lam_init, *lw)
        kp_l.append(kp); vp_l.append(vp); cp_l.append(cp); sp_l.append(sp)
        ks_l.append(kn); vs_l.append(vn); cs_l.append(cn); ss_l.append(sn)
    y_prompt = rmsnorm(hp, final_norm_g)
    y_sample = rmsnorm(hs, final_norm_g)
    return (y_prompt, y_sample, jnp.stack(kp_l), jnp.stack(vp_l), jnp.stack(cp_l), jnp.stack(sp_l),
            jnp.stack(ks_l), jnp.stack(vs_l), jnp.stack(cs_l), jnp.stack(ss_l))
```

```python
import functools
import math

import jax
import jax.numpy as jnp
from jax import lax
from jax.experimental import pallas as pl
from jax.experimental.pallas import tpu as pltpu

D_HEAD = 64
HEAD_W = 2 * D_HEAD
SSM_HEAD_DIM = 64
SSM_GROUPS = 8
SSM_STATE = 128
CONV_K = 4
SSD_CHUNK = 128
NORM_EPS = 1e-6
LANES = 128
SUBLANES = 8
VMEM_LIMIT = 56 * 1024 * 1024

F32 = jnp.float32
BF16 = jnp.bfloat16


def _cparams(sem):
    return pltpu.CompilerParams(dimension_semantics=sem, vmem_limit_bytes=VMEM_LIMIT)


def _rmsnorm_body(x_ref, g_ref, o_ref):
    x = x_ref[...]
    ms = jnp.mean(x * x, axis=-1, keepdims=True)
    o_ref[...] = (x * lax.rsqrt(ms + NORM_EPS) * g_ref[...]).astype(o_ref.dtype)


def rmsnorm(x, g, out_dtype, rows=256):
    m, d = x.shape
    rows = min(rows, m)
    return pl.pallas_call(
        _rmsnorm_body,
        grid=(m // rows,),
        in_specs=[pl.BlockSpec((rows, d), lambda i: (i, 0)),
                  pl.BlockSpec((1, d), lambda i: (0, 0))],
        out_specs=pl.BlockSpec((rows, d), lambda i: (i, 0)),
        out_shape=jax.ShapeDtypeStruct((m, d), out_dtype),
        compiler_params=_cparams(("parallel",)),
        name="rmsnorm",
    )(x, g.reshape(1, d))


def _mm_body(*refs, nk, act, scale, has_res):
    if has_res:
        a_ref, b_ref, r_ref, o_ref = refs[:4]
        scr = refs[4:]
    else:
        a_ref, b_ref, o_ref = refs[:3]
        r_ref = None
        scr = refs[3:]

    def finish(acc):
        if act == "relu2":
            acc = jnp.square(jnp.maximum(acc, 0.0))
        if scale != 1.0:
            acc = acc * scale
        if r_ref is not None:
            acc = acc + r_ref[...]
        o_ref[...] = acc.astype(o_ref.dtype)

    part = jnp.dot(a_ref[...], b_ref[...], preferred_element_type=F32)
    if nk == 1:
        finish(part)
    else:
        acc_ref = scr[0]
        k = pl.program_id(2)

        @pl.when(k == 0)
        def _():
            acc_ref[...] = part

        @pl.when(jnp.logical_and(k > 0, k < nk - 1))
        def _():
            acc_ref[...] += part

        @pl.when(k == nk - 1)
        def _():
            finish(acc_ref[...] + part)


def matmul(a, b, *, n_off=0, n=None, out_dtype=F32, act=None, scale=1.0, res=None,
           tm=1024, tn=1024, tk=4096):
    m, kdim = a.shape
    n = b.shape[1] - n_off if n is None else n
    tm, tn, tk = min(tm, m), min(tn, n), min(tk, kdim)
    assert m % tm == 0 and n % tn == 0 and kdim % tk == 0 and n_off % tn == 0
    nk = kdim // tk
    joff = n_off // tn
    in_specs = [pl.BlockSpec((tm, tk), lambda i, j, k: (i, k)),
                pl.BlockSpec((tk, tn), lambda i, j, k: (k, j + joff))]
    args = [a, b]
    if res is not None:
        in_specs.append(pl.BlockSpec((tm, tn), lambda i, j, k: (i, j)))
        args.append(res)
    return pl.pallas_call(
        functools.partial(_mm_body, nk=nk, act=act, scale=scale, has_res=res is not None),
        grid=(m // tm, n // tn, nk),
        in_specs=in_specs,
        out_specs=pl.BlockSpec((tm, tn), lambda i, j, k: (i, j)),
        out_shape=jax.ShapeDtypeStruct((m, n), out_dtype),
        scratch_shapes=[pltpu.VMEM((tm, tn), F32)] if nk > 1 else [],
        compiler_params=_cparams(("parallel", "parallel", "arbitrary")),
        name="matmul",
    )(*args)


def _lam_value(lq_ref, lam_init):
    lq = lq_ref[...]
    a = jnp.sum(lq[0:1] * lq[1:2], axis=-1, keepdims=True)
    b = jnp.sum(lq[2:3] * lq[3:4], axis=-1, keepdims=True)
    return jnp.exp(a) - jnp.exp(b) + lam_init


def _head_norm(o, g, lam_init):
    ms = jnp.mean(o * o, axis=-1, keepdims=True)
    return o * lax.rsqrt(ms + NORM_EPS) * g * (1.0 - lam_init)


def _attn_prompt_body(q_ref, k_ref, v_ref, lq_ref, g_ref, o_ref,
                      kb_ref, vb_ref, m_ref, l_ref, acc_ref, *, bq, lam_init):
    i = pl.program_id(2)

    @pl.when(i == 0)
    def _():
        kb_ref[...] = k_ref[...].astype(BF16)
        vb_ref[...] = v_ref[...].astype(BF16)

    lam = _lam_value(lq_ref, lam_init)
    q = q_ref[...]
    lane = lax.broadcasted_iota(jnp.int32, q.shape, 1)
    zero = jnp.zeros_like(q)
    q2 = jnp.concatenate([jnp.where(lane < D_HEAD, q, zero), jnp.where(lane < D_HEAD, zero, q)], axis=0)

    m_ref[...] = jnp.full(m_ref.shape, -jnp.inf, F32)
    l_ref[...] = jnp.zeros(l_ref.shape, F32)
    acc_ref[...] = jnp.zeros(acc_ref.shape, F32)

    def step(j, masked):
        start = pl.multiple_of(j * bq, bq)
        kb = kb_ref[pl.ds(start, bq), :]
        vb = vb_ref[pl.ds(start, bq), :]
        s = lax.dot_general(q2, kb, (((1,), (1,)), ((), ())), preferred_element_type=F32)
        if masked:
            row = lax.broadcasted_iota(jnp.int32, (bq, bq), 0)
            col = lax.broadcasted_iota(jnp.int32, (bq, bq), 1)
            keep = jnp.concatenate([row >= col, row >= col], axis=0)
            s = jnp.where(keep, s, -jnp.inf)
        m_old = m_ref[...]
        m_new = jnp.maximum(m_old, jnp.max(s, axis=-1, keepdims=True))
        alpha = jnp.exp(m_old - m_new)
        p = jnp.exp(s - m_new)
        l_ref[...] = alpha * l_ref[...] + jnp.sum(p, axis=-1, keepdims=True)
        acc_ref[...] = alpha * acc_ref[...] + jnp.dot(p.astype(BF16), vb, preferred_element_type=F32)
        m_ref[...] = m_new

    def body(j, carry):
        step(j, False)
        return carry

    lax.fori_loop(0, i, body, 0)
    step(i, True)

    d = acc_ref[...] / l_ref[...]
    o = d[:bq] - lam * d[bq:]
    o_ref[...] = _head_norm(o, g_ref[...], lam_init).astype(o_ref.dtype)


def attn_prompt(q, k, v, lambda_qk, subln_g, lam_init, batch, seq, bq=256):
    m, width = q.shape
    nh = width // HEAD_W
    nq = seq // bq
    return pl.pallas_call(
        functools.partial(_attn_prompt_body, bq=bq, lam_init=lam_init),
        grid=(batch, nh, nq),
        in_specs=[pl.BlockSpec((bq, HEAD_W), lambda b, h, i: (b * nq + i, h)),
                  pl.BlockSpec((seq, HEAD_W), lambda b, h, i: (b, h)),
                  pl.BlockSpec((seq, HEAD_W), lambda b, h, i: (b, h)),
                  pl.BlockSpec((4, D_HEAD), lambda b, h, i: (0, 0)),
                  pl.BlockSpec((1, HEAD_W), lambda b, h, i: (0, 0))],
        out_specs=pl.BlockSpec((bq, HEAD_W), lambda b, h, i: (b * nq + i, h)),
        out_shape=jax.ShapeDtypeStruct((m, width), BF16),
        scratch_shapes=[pltpu.VMEM((seq, HEAD_W), BF16), pltpu.VMEM((seq, HEAD_W), BF16),
                        pltpu.VMEM((2 * bq, 1), F32), pltpu.VMEM((2 * bq, 1), F32),
                        pltpu.VMEM((2 * bq, HEAD_W), F32)],
        compiler_params=_cparams(("parallel", "parallel", "arbitrary")),
        name="attn_prompt",
    )(q, k, v, lambda_qk, subln_g.reshape(1, HEAD_W))


def _attn_sample_body(pt_ref, q_ref, kc_ref, vc_ref, kn_ref, vn_ref, lq_ref, g_ref, o_ref,
                      m_ref, l_ref, acc_ref, *, n_pages, nh, n_new, lam_init):
    p = pl.program_id(1)
    rows = q_ref.shape[0]
    q = (q_ref[...] * (D_HEAD ** -0.5)).astype(BF16)

    @pl.when(p == 0)
    def _():
        m_ref[...] = jnp.full(m_ref.shape, -jnp.inf, F32)
        l_ref[...] = jnp.zeros(l_ref.shape, F32)
        acc_ref[...] = jnp.zeros(acc_ref.shape, F32)

    def update(kf, vf, keep):
        s = lax.dot_general(q, kf.astype(BF16), (((1,), (1,)), ((), ())), preferred_element_type=F32)
        s = jnp.where(keep, s, -jnp.inf)
        m_old = m_ref[...]
        m_new = jnp.maximum(m_old, jnp.max(s, axis=-1, keepdims=True))
        alpha = jnp.exp(m_old - m_new)
        pr = jnp.exp(s - m_new)
        l_ref[...] = alpha * l_ref[...] + jnp.sum(pr, axis=-1, keepdims=True)
        acc_ref[...] = alpha * acc_ref[...] + jnp.dot(pr.astype(BF16), vf.astype(BF16),
                                                      preferred_element_type=F32)
        m_ref[...] = m_new

    @pl.when(p < n_pages)
    def _():
        ntok = kc_ref.shape[0]
        cols = ntok * nh
        r = lax.broadcasted_iota(jnp.int32, (rows, cols), 0)
        c = lax.broadcasted_iota(jnp.int32, (rows, cols), 1)
        keep = (c % nh) == (r // 8)
        update(kc_ref[...].reshape(cols, HEAD_W), vc_ref[...].reshape(cols, HEAD_W), keep)

    @pl.when(p == n_pages)
    def _():
        cols = kn_ref.shape[0]
        r = lax.broadcasted_iota(jnp.int32, (rows, cols), 0)
        c = lax.broadcasted_iota(jnp.int32, (rows, cols), 1)
        tok = c // nh
        keep = jnp.logical_and((c % nh) == (r // 8), tok <= (r % n_new))
        update(kn_ref[...], vn_ref[...], keep)
        lam = _lam_value(lq_ref, lam_init)
        d = acc_ref[...] / l_ref[...]
        d1 = pltpu.roll(d, rows - n_new, axis=0)
        o = d - lam * d1
        o_ref[...] = _head_norm(o, g_ref[...], lam_init).astype(o_ref.dtype)


def attn_sample(qrows, cache_k, cache_v, knew, vnew, page_table, lambda_qk, subln_g, lam_init, layer, n_new):
    nb, rows, _ = qrows.shape
    nh = rows // 8
    n_pages = page_table.shape[1]
    page = cache_k.shape[2]
    last = n_pages - 1

    def cache_map(b, p, pt):
        return (layer, pt[b, jnp.minimum(p, last)], 0, 0, 0)

    grid_spec = pltpu.PrefetchScalarGridSpec(
        num_scalar_prefetch=1,
        grid=(nb, n_pages + 1),
        in_specs=[pl.BlockSpec((None, rows, HEAD_W), lambda b, p, pt: (b, 0, 0)),
                  pl.BlockSpec((None, None, page, nh, HEAD_W), cache_map),
                  pl.BlockSpec((None, None, page, nh, HEAD_W), cache_map),
                  pl.BlockSpec((None, 8 * nh, HEAD_W), lambda b, p, pt: (b, 0, 0)),
                  pl.BlockSpec((None, 8 * nh, HEAD_W), lambda b, p, pt: (b, 0, 0)),
                  pl.BlockSpec((4, D_HEAD), lambda b, p, pt: (0, 0)),
                  pl.BlockSpec((1, HEAD_W), lambda b, p, pt: (0, 0))],
        out_specs=pl.BlockSpec((None, rows, HEAD_W), lambda b, p, pt: (b, 0, 0)),
        scratch_shapes=[pltpu.VMEM((rows, 1), F32), pltpu.VMEM((rows, 1), F32),
                        pltpu.VMEM((rows, HEAD_W), F32)],
    )
    return pl.pallas_call(
        functools.partial(_attn_sample_body, n_pages=n_pages, nh=nh, n_new=n_new, lam_init=lam_init),
        grid_spec=grid_spec,
        out_shape=jax.ShapeDtypeStruct((nb, rows, HEAD_W), F32),
        compiler_params=_cparams(("parallel", "arbitrary")),
        name="attn_sample",
    )(page_table, qrows, cache_k, cache_v, knew, vnew, lambda_qk, subln_g.reshape(1, HEAD_W))


def _split3(x):
    hi = x.astype(BF16)
    r1 = x - hi.astype(F32)
    mid = r1.astype(BF16)
    lo = (r1 - mid.astype(F32)).astype(BF16)
    return hi, mid, lo


def _silu(x):
    return x * (1.0 / (1.0 + jnp.exp(-x)))


def _ssd_body(xbc_ref, z_ref, dtr_ref, cprev_ref, h0_ref, cw_ref, cb_ref, dtb_ref, alog_ref,
              dskip_ref, ng_ref, e64_ref, et_ref, y_ref, hout_ref,
              tail_ref, st_ref, xpad_ref, *, valid, nheads):
    c = pl.program_id(1)
    nc = pl.num_programs(1)
    T = xbc_ref.shape[0]
    width = z_ref.shape[1]
    gn = SSM_GROUPS * SSM_STATE
    hpg = nheads // SSM_GROUPS
    gw = hpg * SSM_HEAD_DIM
    ntile = width // LANES

    @pl.when(c == 0)
    def _():
        tail_ref[...] = cprev_ref[...]
        h0 = h0_ref[...].reshape(width, SSM_STATE)
        for j in range(ntile):
            st_ref[:, j * LANES:(j + 1) * LANES] = h0[j * LANES:(j + 1) * LANES, :].T

    xbc = xbc_ref[...]
    xpad_ref[0:SUBLANES, :] = tail_ref[...]
    xpad_ref[SUBLANES:SUBLANES + T, :] = xbc
    tail_ref[...] = xbc[T - SUBLANES:, :]
    acc = cb_ref[...] + xbc * cw_ref[CONV_K - 1:CONV_K, :]
    for j in range(CONV_K - 1):
        off = SUBLANES - (CONV_K - 1) + j
        acc = acc + xpad_ref[off:off + T, :] * cw_ref[j:j + 1, :]
    xc = _silu(acc)
    xs = xc[:, :width]
    bm = xc[:, width:width + gn].astype(BF16)
    cm = xc[:, width + gn:].astype(BF16)

    lane = lax.broadcasted_iota(jnp.int32, (T, LANES), 1)
    row = lax.broadcasted_iota(jnp.int32, (T, LANES), 0)
    live = jnp.logical_and(lane < nheads, row < valid)
    xdt = dtr_ref[...] + dtb_ref[...]
    dt = jnp.where(live, jnp.maximum(xdt, 0.0) + jnp.log1p(jnp.exp(-jnp.abs(xdt))), 0.0)
    da = dt * (-jnp.exp(alog_ref[...]))
    tr = lax.broadcasted_iota(jnp.int32, (T, T), 0)
    tc = lax.broadcasted_iota(jnp.int32, (T, T), 1)
    causal = tr >= tc
    ltri = jnp.where(causal, 1.0, 0.0).astype(BF16)
    cs = jnp.dot(jnp.concatenate([ltri, ltri, ltri], axis=1),
                 jnp.concatenate(_split3(da), axis=0), preferred_element_type=F32)
    cs_t = cs.T

    stack = jnp.concatenate([dt, cs], axis=0)
    exp64 = jnp.dot(jnp.concatenate(_split3(stack), axis=1), e64_ref[...], preferred_element_type=F32)
    dt_x = exp64[:T]
    cs_x = exp64[T:]
    cs_b = jnp.dot(jnp.concatenate(_split3(cs), axis=1), et_ref[...], preferred_element_type=F32)

    cs_last = cs_x[T - 1:T, :]
    ecs_x = jnp.exp(cs_x)
    dte_x = jnp.exp(cs_last - cs_x)
    x = xs * dt_x
    xb = x.astype(BF16)
    xdb = (x * dte_x).astype(BF16)

    lane2 = lax.broadcasted_iota(jnp.int32, (T, LANES), 1)
    y_parts = []
    for g in range(SSM_GROUPS):
        bg = bm[:, g * SSM_STATE:(g + 1) * SSM_STATE]
        cg = cm[:, g * SSM_STATE:(g + 1) * SSM_STATE]
        cb = lax.dot_general(cg, bg, (((1,), (1,)), ((), ())), preferred_element_type=F32)
        st_g = st_ref[:, g * gw:(g + 1) * gw]
        y_off = jnp.dot(cg, st_g.astype(BF16), preferred_element_type=F32) * ecs_x[:, g * gw:(g + 1) * gw]
        y_diag = []
        for pr in range(gw // LANES):
            mats = []
            for hh in range(LANES // SSM_HEAD_DIM):
                h = g * hpg + pr * (LANES // SSM_HEAD_DIM) + hh
                seg = cs_b[:, h * T:(h + 1) * T] - cs_t[h:h + 1, :]
                lmat = jnp.exp(jnp.where(causal, seg, -jnp.inf))
                mats.append((cb * lmat).astype(BF16))
            tile = g * (gw // LANES) + pr
            x2 = xb[:, tile * LANES:(tile + 1) * LANES]
            zero = jnp.zeros_like(x2)
            xbd = jnp.concatenate([jnp.where(lane2 < SSM_HEAD_DIM, x2, zero),
                                   jnp.where(lane2 < SSM_HEAD_DIM, zero, x2)], axis=0)
            y_diag.append(jnp.dot(jnp.concatenate(mats, axis=1), xbd, preferred_element_type=F32))
        y_parts.append(jnp.concatenate(y_diag, axis=1) + y_off)
        bg_t = xc[:, width + g * SSM_STATE:width + (g + 1) * SSM_STATE].T.astype(BF16)
        new_states = jnp.dot(bg_t, xdb[:, g * gw:(g + 1) * gw], preferred_element_type=F32)
        decay = ecs_x[T - 1:T, g * gw:(g + 1) * gw]
        st_ref[:, g * gw:(g + 1) * gw] = st_g * decay + new_states

    y = jnp.concatenate(y_parts, axis=1) + dskip_ref[...] * xs
    y = y * _silu(z_ref[...])
    outs = []
    for g in range(SSM_GROUPS):
        yg = y[:, g * gw:(g + 1) * gw]
        ms = jnp.mean(yg * yg, axis=-1, keepdims=True)
        outs.append(yg * lax.rsqrt(ms + NORM_EPS))
    y_ref[...] = (jnp.concatenate(outs, axis=1) * ng_ref[...]).astype(y_ref.dtype)

    @pl.when(c == nc - 1)
    def _():
        for j in range(ntile):
            hout_ref[j * LANES // SSM_HEAD_DIM:(j + 1) * LANES // SSM_HEAD_DIM] = (
                st_ref[:, j * LANES:(j + 1) * LANES].T.reshape(LANES // SSM_HEAD_DIM, SSM_HEAD_DIM, SSM_STATE))


def ssd_mixer(xbc, z, dtr, conv_prev, h0, conv_w, conv_b, dt_bias, a_log, d_skip, norm_g,
              batch, valid):
    mtot, conv_dim = xbc.shape
    width = z.shape[1]
    nheads = width // SSM_HEAD_DIM
    T = SSD_CHUNK
    nc = mtot // batch // T
    hid = jnp.arange(LANES)[:, None]
    e64 = (jnp.arange(width)[None, :] // SSM_HEAD_DIM == hid).astype(BF16)
    et = (jnp.arange(nheads * T)[None, :] // T == hid).astype(BF16)
    e64 = jnp.concatenate([e64] * 3, axis=0)
    et = jnp.concatenate([et] * 3, axis=0)
    pad_l = lambda v: jnp.pad(v.astype(F32), (0, LANES - nheads)).reshape(1, LANES)
    cw = jnp.pad(conv_w, ((0, SUBLANES - CONV_K), (0, 0)))
    full = lambda shape: pl.BlockSpec(shape, lambda b, c: (0,) * len(shape))
    y, hout = pl.pallas_call(
        functools.partial(_ssd_body, valid=valid, nheads=nheads),
        grid=(batch, nc),
        in_specs=[pl.BlockSpec((T, conv_dim), lambda b, c: (b * nc + c, 0)),
                  pl.BlockSpec((T, width), lambda b, c: (b * nc + c, 0)),
                  pl.BlockSpec((T, LANES), lambda b, c: (b * nc + c, 0)),
                  pl.BlockSpec((None, SUBLANES, conv_dim), lambda b, c: (b, 0, 0)),
                  pl.BlockSpec((None, nheads, SSM_HEAD_DIM, SSM_STATE), lambda b, c: (b, 0, 0, 0)),
                  full((SUBLANES, conv_dim)), full((1, conv_dim)), full((1, LANES)), full((1, LANES)),
                  full((1, width)), full((1, width)), full((3 * LANES, width)), full((3 * LANES, nheads * T))],
        out_specs=[pl.BlockSpec((T, width), lambda b, c: (b * nc + c, 0)),
                   pl.BlockSpec((None, nheads, SSM_HEAD_DIM, SSM_STATE), lambda b, c: (b, 0, 0, 0))],
        out_shape=[jax.ShapeDtypeStruct((mtot, width), BF16),
                   jax.ShapeDtypeStruct((batch, nheads, SSM_HEAD_DIM, SSM_STATE), F32)],
        scratch_shapes=[pltpu.VMEM((SUBLANES, conv_dim), F32),
                        pltpu.VMEM((SSM_STATE, width), F32),
                        pltpu.VMEM((SUBLANES + T, conv_dim), F32)],
        compiler_params=_cparams(("parallel", "arbitrary")),
        name="ssd_mixer",
    )(xbc, z, dtr, conv_prev, h0, cw, conv_b.reshape(1, conv_dim), pad_l(dt_bias), pad_l(a_log),
      jnp.repeat(d_skip.astype(F32), SSM_HEAD_DIM).reshape(1, width), norm_g.reshape(1, width), e64, et)
    return y, hout


def _layer(x_p, x_s, l, lam_init, dims, cache_k, cache_v, state_conv, state_ssm, page_table, w):
    bp, seq, db, dseq = dims
    att_w, ssm_w, conv_dim, nheads = w["att_w"], w["ssm_w"], w["conv_dim"], w["nheads"]
    nh = att_w // HEAD_W
    qscale = D_HEAD ** -0.5
    w_in, w_dt = w["w_in"][l], w["w_dt"][l]
    offs = {"q": 0, "k": att_w, "v": 2 * att_w, "z": 3 * att_w, "xbc": 3 * att_w + ssm_w}

    def in_proj(x):
        u = rmsnorm(x, w["attn_norm_g"][l], BF16)
        mm = functools.partial(matmul, u, w_in)
        k = mm(n_off=offs["k"], n=att_w)
        v = mm(n_off=offs["v"], n=att_w)
        z = mm(n_off=offs["z"], n=ssm_w)
        xbc = mm(n_off=offs["xbc"], n=conv_dim)
        dtr = matmul(u, w_dt)
        return u, k, v, z, xbc, dtr

    def tail(x, mix):
        h = matmul(mix, w["w_out"][l], res=x)
        m = rmsnorm(h, w["mlp_norm_g"][l], BF16)
        a = matmul(m, w["w_up"][l], out_dtype=BF16, act="relu2")
        return matmul(a, w["w_down"][l], res=h, tk=2048)

    ssm_args = (w["conv_w"][l], w["conv_b"][l], w["dt_bias"][l], w["a_log"][l], w["d_skip"][l],
                w["ssm_norm_g"][l])

    u, k_p, v_p, z, xbc, dtr = in_proj(x_p)
    q = matmul(u, w_in, n_off=offs["q"], n=att_w, out_dtype=BF16, scale=qscale)
    o = attn_prompt(q, k_p, v_p, w["lambda_qk"][l], w["subln_g"][l], lam_init, bp, seq)
    y, ssm_p = ssd_mixer(xbc, z, dtr, jnp.zeros((bp, SUBLANES, conv_dim), F32),
                         jnp.zeros((bp, nheads, SSM_HEAD_DIM, SSM_STATE), F32), *ssm_args,
                         batch=bp, valid=SSD_CHUNK)
    conv_p = xbc.reshape(bp, seq, conv_dim)[:, seq - (CONV_K - 1):]
    x_p = tail(x_p, jnp.concatenate([o, y], axis=1))

    u, k_s, v_s, z, xbc, dtr = in_proj(x_s)
    q = matmul(u, w_in, n_off=offs["q"], n=att_w)
    q5 = q.reshape(db, dseq, nh, 2, D_HEAD).transpose(0, 2, 1, 3, 4)
    sel = jnp.eye(2, dtype=F32)[None, None, :, None, :, None]
    qrows = (q5[:, :, None] * sel).reshape(db, nh * 8, HEAD_W)

    def tok_rows(t):
        t = t.reshape(db, dseq, nh, HEAD_W)
        return jnp.pad(t, ((0, 0), (0, SUBLANES - dseq), (0, 0), (0, 0))).reshape(db, SUBLANES * nh, HEAD_W)

    o = attn_sample(qrows, cache_k, cache_v, tok_rows(k_s), tok_rows(v_s), page_table,
                    w["lambda_qk"][l], w["subln_g"][l], lam_init, l, dseq)
    o = o.reshape(db, nh, 8, HEAD_W)[:, :, :dseq].transpose(0, 2, 1, 3).reshape(db * dseq, att_w)

    def pad_chunk(t):
        t = t.reshape(db, dseq, t.shape[-1])
        return jnp.pad(t, ((0, 0), (0, SSD_CHUNK - dseq), (0, 0))).reshape(db * SSD_CHUNK, t.shape[-1])

    cprev = jnp.pad(state_conv[l], ((0, 0), (SUBLANES - (CONV_K - 1), 0), (0, 0)))
    y, ssm_s = ssd_mixer(pad_chunk(xbc), pad_chunk(z), pad_chunk(dtr), cprev, state_ssm[l], *ssm_args,
                         batch=db, valid=dseq)
    y = y.reshape(db, SSD_CHUNK, ssm_w)[:, :dseq].reshape(db * dseq, ssm_w)
    conv_s = jnp.concatenate([state_conv[l], xbc.reshape(db, dseq, conv_dim)], axis=1)[:, -(CONV_K - 1):]
    x_s = tail(x_s, jnp.concatenate([o.astype(BF16), y], axis=1))

    outs = (k_p.reshape(bp, seq, nh, HEAD_W), v_p.reshape(bp, seq, nh, HEAD_W), conv_p, ssm_p,
            k_s.reshape(db, dseq, nh, HEAD_W), v_s.reshape(db, dseq, nh, HEAD_W), conv_s, ssm_s)
    return x_p, x_s, outs


def kernel(x_prompt, x_sample, cache_k, cache_v, state_conv, state_ssm, page_table, attn_norm_g, w_in,
           lambda_qk, subln_g, conv_w, conv_b, dt_bias, a_log, d_skip, ssm_norm_g, w_out, mlp_norm_g,
           w_up, w_down, final_norm_g):
    bp, seq, d_model = x_prompt.shape
    db, dseq, _ = x_sample.shape
    depth = w_in.shape[0]
    nheads = dt_bias.shape[1]
    ssm_w = ssm_norm_g.shape[1]
    att_w = d_model - ssm_w
    conv_dim = conv_w.shape[2]
    main_cols = 3 * att_w + ssm_w + conv_dim
    w = dict(att_w=att_w, ssm_w=ssm_w, conv_dim=conv_dim, nheads=nheads,
             w_in=w_in.astype(BF16),
             w_dt=jnp.pad(w_in[:, :, main_cols:], ((0, 0), (0, 0), (0, LANES - nheads))).astype(BF16),
             w_out=w_out.astype(BF16), w_up=w_up.astype(BF16), w_down=w_down.astype(BF16),
             attn_norm_g=attn_norm_g, mlp_norm_g=mlp_norm_g, lambda_qk=lambda_qk, subln_g=subln_g,
             conv_w=conv_w, conv_b=conv_b, dt_bias=dt_bias, a_log=a_log, d_skip=d_skip, ssm_norm_g=ssm_norm_g)
    x_p = x_prompt.reshape(bp * seq, d_model)
    x_s = x_sample.reshape(db * dseq, d_model)
    per_layer = []
    for l in range(depth):
        lam_init = 0.8 - 0.6 * math.exp(-0.3 * l)
        x_p, x_s, outs = _layer(x_p, x_s, l, lam_init, (bp, seq, db, dseq), cache_k, cache_v,
                                state_conv, state_ssm, page_table, w)
        per_layer.append(outs)
    y_p = rmsnorm(x_p, final_norm_g, F32).reshape(bp, seq, d_model)
    y_s = rmsnorm(x_s, final_norm_g, F32).reshape(db, dseq, d_model)
    stacked = [jnp.stack([per_layer[l][i] for l in range(depth)]) for i in range(8)]
    return (y_p, y_s, *stacked)
```

```python
import functools
import math

import jax
import jax.numpy as jnp
from jax import lax
from jax.experimental import pallas as pl
from jax.experimental.pallas import tpu as pltpu

D_HEAD = 64
HEAD_W = 2 * D_HEAD
SSM_HEAD_DIM = 64
SSM_GROUPS = 8
SSM_STATE = 128
CONV_K = 4
SSD_CHUNK = 128
NORM_EPS = 1e-6
LANES = 128
SUBLANES = 8
VMEM_LIMIT = 56 * 1024 * 1024
PAGES_PER_STEP = 4

F32 = jnp.float32
BF16 = jnp.bfloat16


def _cparams(sem):
    return pltpu.CompilerParams(dimension_semantics=sem, vmem_limit_bytes=VMEM_LIMIT)


def _rmsnorm_body(x_ref, g_ref, o_ref):
    x = x_ref[...]
    ms = jnp.mean(x * x, axis=-1, keepdims=True)
    o_ref[...] = (x * lax.rsqrt(ms + NORM_EPS) * g_ref[...]).astype(o_ref.dtype)


def rmsnorm(x, g, out_dtype, rows=256):
    m, d = x.shape
    rows = min(rows, m)
    return pl.pallas_call(
        _rmsnorm_body,
        grid=(m // rows,),
        in_specs=[pl.BlockSpec((rows, d), lambda i: (i, 0)),
                  pl.BlockSpec((1, d), lambda i: (0, 0))],
        out_specs=pl.BlockSpec((rows, d), lambda i: (i, 0)),
        out_shape=jax.ShapeDtypeStruct((m, d), out_dtype),
        compiler_params=_cparams(("parallel",)),
        name="rmsnorm",
    )(x, g.reshape(1, d))


def _mm_body(*refs, n_a, nk, act, scale, has_res, has_buf):
    a_refs, b_ref = refs[:n_a], refs[n_a]
    pos = n_a + 1
    r_ref = refs[pos] if has_res else None
    pos += int(has_res) + int(has_buf)
    o_ref, scr = refs[pos], refs[pos + 1:]

    def finish(acc):
        if act == "relu2":
            acc = jnp.square(jnp.maximum(acc, 0.0))
        if scale != 1.0:
            acc = acc * scale
        if r_ref is not None:
            acc = acc + r_ref[...]
        o_ref[...] = acc.astype(o_ref.dtype)

    def part(a_ref):
        return jnp.dot(a_ref[...], b_ref[...], preferred_element_type=F32)

    if nk == 1:
        finish(part(a_refs[0]))
        return
    acc_ref = scr[0]
    k = pl.program_id(2)
    if n_a == 1:
        p = part(a_refs[0])

        @pl.when(k == 0)
        def _():
            acc_ref[...] = p

        @pl.when(jnp.logical_and(k > 0, k < nk - 1))
        def _():
            acc_ref[...] += p

        @pl.when(k == nk - 1)
        def _():
            finish(acc_ref[...] + p)
    else:
        for idx, a_ref in enumerate(a_refs):
            @pl.when(k == idx)
            def _(a_ref=a_ref, idx=idx):
                if idx == 0:
                    acc_ref[...] = part(a_ref)
                elif idx < nk - 1:
                    acc_ref[...] += part(a_ref)
                else:
                    finish(acc_ref[...] + part(a_ref))


def matmul(a, w, layer, *, n_off=0, n=None, out_dtype=F32, act=None, scale=1.0, res=None, out_buf=None,
           tm=1024, tn=1024, tk=4096):
    a_list = a if isinstance(a, tuple) else (a,)
    m = a_list[0].shape[0]
    kdim = sum(x.shape[1] for x in a_list)
    n = w.shape[2] - n_off if n is None else n
    tm, tn = min(tm, m), min(tn, n)
    tk = a_list[0].shape[1] if len(a_list) > 1 else min(tk, kdim)
    assert m % tm == 0 and n % tn == 0 and kdim % tk == 0 and n_off % tn == 0
    assert all(x.shape[1] == tk for x in a_list) or len(a_list) == 1
    nk = kdim // tk
    joff = n_off // tn
    if len(a_list) > 1:
        in_specs = [pl.BlockSpec((tm, tk), lambda i, j, k: (i, 0)) for _ in a_list]
    else:
        in_specs = [pl.BlockSpec((tm, tk), lambda i, j, k: (i, k))]
    in_specs.append(pl.BlockSpec((None, tk, tn), lambda i, j, k: (layer, k, j + joff)))
    args = list(a_list) + [w]
    if res is not None:
        in_specs.append(pl.BlockSpec((tm, tn), lambda i, j, k: (i, j)))
        args.append(res)
    aliases = {}
    if out_buf is not None:
        in_specs.append(pl.BlockSpec(memory_space=pl.ANY))
        aliases = {len(args): 0}
        args.append(out_buf)
        out_spec = pl.BlockSpec((None, tm, tn), lambda i, j, k: (layer, i, j))
        out_shape = jax.ShapeDtypeStruct(out_buf.shape, out_buf.dtype)
    else:
        out_spec = pl.BlockSpec((tm, tn), lambda i, j, k: (i, j))
        out_shape = jax.ShapeDtypeStruct((m, n), out_dtype)
    return pl.pallas_call(
        functools.partial(_mm_body, n_a=len(a_list), nk=nk, act=act, scale=scale,
                          has_res=res is not None, has_buf=out_buf is not None),
        grid=(m // tm, n // tn, nk),
        in_specs=in_specs,
        out_specs=out_spec,
        out_shape=out_shape,
        scratch_shapes=[pltpu.VMEM((tm, tn), F32)] if nk > 1 else [],
        input_output_aliases=aliases,
        compiler_params=_cparams(("parallel", "parallel", "arbitrary")),
        name="matmul",
    )(*args)


def _lam_value(lq_ref, lam_init):
    lq = lq_ref[...]
    a = jnp.sum(lq[0:1] * lq[1:2], axis=-1, keepdims=True)
    b = jnp.sum(lq[2:3] * lq[3:4], axis=-1, keepdims=True)
    return jnp.exp(a) - jnp.exp(b) + lam_init


def _head_norm(o, g, lam_init):
    ms = jnp.mean(o * o, axis=-1, keepdims=True)
    return o * lax.rsqrt(ms + NORM_EPS) * g * (1.0 - lam_init)


def _attn_prompt_body(q_ref, k_ref, v_ref, lq_ref, g_ref, o_ref, kb_ref, vt_ref, st_ref, acc_ref, l_ref,
                      *, bq, lam_init):
    seq = q_ref.shape[0]
    nq = seq // bq
    kb_ref[...] = k_ref[...].astype(BF16)
    for j in range(nq):
        vt_ref[j] = v_ref[j * bq:(j + 1) * bq, :].T.astype(BF16)
    lam = _lam_value(lq_ref, lam_init)
    sub = lax.broadcasted_iota(jnp.int32, (HEAD_W, bq), 0)
    key_row = lax.broadcasted_iota(jnp.int32, (bq, 2 * bq), 0)
    q_col = lax.broadcasted_iota(jnp.int32, (bq, 2 * bq), 1)
    keep = key_row <= jnp.where(q_col >= bq, q_col - bq, q_col)

    def fold(x):
        return x.reshape(bq // SUBLANES, SUBLANES, 2 * bq)

    def qblock(i, carry):
        qs = pl.multiple_of(i * bq, bq)
        qt = q_ref[pl.ds(qs, bq), :].T
        zero = jnp.zeros_like(qt)
        q2t = jnp.concatenate([jnp.where(sub < D_HEAD, qt, zero), jnp.where(sub < D_HEAD, zero, qt)],
                              axis=1).astype(BF16)

        def scores(j):
            ks = pl.multiple_of(j * bq, bq)
            return ks, jnp.dot(kb_ref[pl.ds(ks, bq), :], q2t, preferred_element_type=F32)

        def pass_a(j, m_run):
            ks, s = scores(j)
            st_ref[pl.ds(ks, bq), :] = s
            return jnp.maximum(m_run, jnp.max(fold(s), axis=0))

        m_run = lax.fori_loop(0, i, pass_a, jnp.full((SUBLANES, 2 * bq), -jnp.inf, F32))
        _, s = scores(i)
        s = jnp.where(keep, s, -jnp.inf)
        st_ref[pl.ds(qs, bq), :] = s
        m_run = jnp.maximum(m_run, jnp.max(fold(s), axis=0))
        m = jnp.max(m_run, axis=0, keepdims=True)

        acc_ref[...] = jnp.zeros(acc_ref.shape, F32)
        l_ref[...] = jnp.zeros(l_ref.shape, F32)

        def pass_b(j, c):
            ks = pl.multiple_of(j * bq, bq)
            p = jnp.exp(st_ref[pl.ds(ks, bq), :] - m)
            l_ref[...] += jnp.sum(fold(p), axis=0)
            acc_ref[...] += jnp.dot(vt_ref[j], p.astype(BF16), preferred_element_type=F32)
            return c

        lax.fori_loop(0, i + 1, pass_b, 0)
        l = jnp.sum(l_ref[...], axis=0, keepdims=True)
        d = acc_ref[...] * (1.0 / l)
        o = (d[:, :bq] - lam * d[:, bq:]).T
        o_ref[pl.ds(qs, bq), :] = _head_norm(o, g_ref[...], lam_init).astype(o_ref.dtype)
        return carry

    lax.fori_loop(0, nq, qblock, 0)


def attn_prompt(q, k_all, v_all, layer, lambda_qk, subln_g, lam_init, batch, seq, bq=256):
    m, width = q.shape
    nh = width // HEAD_W
    kv_spec = pl.BlockSpec((None, seq, HEAD_W), lambda b, h: (layer, b, h))
    return pl.pallas_call(
        functools.partial(_attn_prompt_body, bq=bq, lam_init=lam_init),
        grid=(batch, nh),
        in_specs=[pl.BlockSpec((seq, HEAD_W), lambda b, h: (b, h)), kv_spec, kv_spec,
                  pl.BlockSpec((4, D_HEAD), lambda b, h: (0, 0)),
                  pl.BlockSpec((1, HEAD_W), lambda b, h: (0, 0))],
        out_specs=pl.BlockSpec((seq, HEAD_W), lambda b, h: (b, h)),
        out_shape=jax.ShapeDtypeStruct((m, width), BF16),
        scratch_shapes=[pltpu.VMEM((seq, HEAD_W), BF16),
                        pltpu.VMEM((seq // bq, HEAD_W, bq), BF16),
                        pltpu.VMEM((seq, 2 * bq), F32),
                        pltpu.VMEM((HEAD_W, 2 * bq), F32),
                        pltpu.VMEM((SUBLANES, 2 * bq), F32)],
        compiler_params=_cparams(("parallel", "parallel")),
        name="attn_prompt",
    )(q, k_all, v_all, lambda_qk, subln_g.reshape(1, HEAD_W))


def _attn_sample_body(pt_ref, q_ref, bias_ref, *refs, n_steps, npp, nh, n_new, lam_init):
    kc_refs, vc_refs = refs[:npp], refs[npp:2 * npp]
    kn_ref, vn_ref, lq_ref, g_ref, o_ref, m_ref, l_ref, acc_ref = refs[2 * npp:]
    p = pl.program_id(1)
    rows = q_ref.shape[0]
    q = (q_ref[...] * (D_HEAD ** -0.5)).astype(BF16)

    @pl.when(p == 0)
    def _():
        m_ref[...] = jnp.full(m_ref.shape, -jnp.inf, F32)
        l_ref[...] = jnp.zeros(l_ref.shape, F32)
        acc_ref[...] = jnp.zeros(acc_ref.shape, F32)

    def update(kv_pairs, biases):
        s_list = [lax.dot_general(q, kf.astype(BF16), (((1,), (1,)), ((), ())), preferred_element_type=F32) + b
                  for (kf, _), b in zip(kv_pairs, biases)]
        m_old = m_ref[...]
        s_max = functools.reduce(jnp.maximum, [jnp.max(s, axis=-1, keepdims=True) for s in s_list])
        m_new = jnp.maximum(m_old, s_max)
        alpha = jnp.exp(m_old - m_new)
        l_new = alpha * l_ref[...]
        acc = alpha * acc_ref[...]
        for s, (_, vf) in zip(s_list, kv_pairs):
            pr = jnp.exp(s - jnp.concatenate([m_new] * (s.shape[1] // LANES), axis=1))
            l_new = l_new + jnp.sum(pr, axis=-1, keepdims=True)
            acc = acc + jnp.dot(pr.astype(BF16), vf.astype(BF16), preferred_element_type=F32)
        m_ref[...] = m_new
        l_ref[...] = l_new
        acc_ref[...] = acc

    @pl.when(p < n_steps)
    def _():
        cols = kc_refs[0].shape[0] * nh
        bias = bias_ref[...]
        update([(kr[...].reshape(cols, HEAD_W), vr[...].reshape(cols, HEAD_W))
                for kr, vr in zip(kc_refs, vc_refs)], [bias] * npp)

    @pl.when(p == n_steps)
    def _():
        cols = kn_ref.shape[0]
        r = lax.broadcasted_iota(jnp.int32, (rows, cols), 0)
        c = lax.broadcasted_iota(jnp.int32, (rows, cols), 1)
        keep = jnp.logical_and((c % nh) == (r // 8), (c // nh) <= (r % n_new))
        update([(kn_ref[...], vn_ref[...])], [jnp.where(keep, 0.0, -jnp.inf)])
        lam = _lam_value(lq_ref, lam_init)
        d = acc_ref[...] / l_ref[...]
        o = d - lam * pltpu.roll(d, rows - n_new, axis=0)
        o_ref[...] = _head_norm(o, g_ref[...], lam_init).astype(o_ref.dtype)


def attn_sample(qrows, cache_k, cache_v, knew, vnew, page_table, lambda_qk, subln_g, lam_init, layer, n_new):
    nb, rows, _ = qrows.shape
    nh = rows // 8
    assert 2 * n_new == 8
    n_pages = page_table.shape[1]
    page = cache_k.shape[2]
    npp = PAGES_PER_STEP
    assert n_pages % npp == 0
    n_steps = n_pages // npp
    cols = page * nh
    bias = jnp.where((jnp.arange(cols)[None, :] % nh) == (jnp.arange(rows)[:, None] // 8), 0.0, -jnp.inf).astype(F32)

    def cache_spec(i):
        def cache_map(b, p, pt):
            return (layer, pt[b, jnp.minimum(p, n_steps - 1) * npp + i], 0, 0, 0)
        return pl.BlockSpec((None, None, page, nh, HEAD_W), cache_map)

    per_b = lambda shape: pl.BlockSpec((None,) + shape, lambda b, p, pt: (b, 0, 0))
    const = lambda shape: pl.BlockSpec(shape, lambda b, p, pt: (0, 0))
    grid_spec = pltpu.PrefetchScalarGridSpec(
        num_scalar_prefetch=1,
        grid=(nb, n_steps + 1),
        in_specs=[per_b((rows, HEAD_W)), const((rows, cols))]
                 + [cache_spec(i) for i in range(npp)] * 2
                 + [per_b((8 * nh, HEAD_W)), per_b((8 * nh, HEAD_W)), const((4, D_HEAD)), const((1, HEAD_W))],
        out_specs=per_b((rows, HEAD_W)),
        scratch_shapes=[pltpu.VMEM((rows, LANES), F32), pltpu.VMEM((rows, LANES), F32),
                        pltpu.VMEM((rows, HEAD_W), F32)],
    )
    return pl.pallas_call(
        functools.partial(_attn_sample_body, n_steps=n_steps, npp=npp, nh=nh, n_new=n_new, lam_init=lam_init),
        grid_spec=grid_spec,
        out_shape=jax.ShapeDtypeStruct((nb, rows, HEAD_W), F32),
        compiler_params=_cparams(("parallel", "arbitrary")),
        name="attn_sample",
    )(page_table, qrows, bias, *([cache_k] * npp), *([cache_v] * npp), knew, vnew, lambda_qk,
      subln_g.reshape(1, HEAD_W))


def _split3(x):
    hi = x.astype(BF16)
    r1 = x - hi.astype(F32)
    mid = r1.astype(BF16)
    lo = (r1 - mid.astype(F32)).astype(BF16)
    return hi, mid, lo


def _silu(x):
    return x * (1.0 / (1.0 + jnp.exp(-x)))


def _ssd_body(xbc_ref, z_ref, dtr_ref, cprev_ref, h0_ref, cw_ref, cb_ref, dtb_ref, alog_ref,
              dskip_ref, ng_ref, e64_ref, et_ref, y_ref, hout_ref,
              tail_ref, st_ref, xpad_ref, *, valid, nheads):
    c = pl.program_id(1)
    nc = pl.num_programs(1)
    T = xbc_ref.shape[0]
    width = z_ref.shape[1]
    gn = SSM_GROUPS * SSM_STATE
    hpg = nheads // SSM_GROUPS
    gw = hpg * SSM_HEAD_DIM
    ntile = width // LANES

    @pl.when(c == 0)
    def _():
        tail_ref[...] = cprev_ref[...]
        h0 = h0_ref[...].reshape(width, SSM_STATE)
        for j in range(ntile):
            st_ref[:, j * LANES:(j + 1) * LANES] = h0[j * LANES:(j + 1) * LANES, :].T

    xbc = xbc_ref[...]
    xpad_ref[0:SUBLANES, :] = tail_ref[...]
    xpad_ref[SUBLANES:SUBLANES + T, :] = xbc
    tail_ref[...] = xbc[T - SUBLANES:, :]
    acc = cb_ref[...] + xbc * cw_ref[CONV_K - 1:CONV_K, :]
    for j in range(CONV_K - 1):
        off = SUBLANES - (CONV_K - 1) + j
        acc = acc + xpad_ref[off:off + T, :] * cw_ref[j:j + 1, :]
    xc = _silu(acc)
    xs = xc[:, :width]
    bm = xc[:, width:width + gn].astype(BF16)
    cm = xc[:, width + gn:].astype(BF16)

    lane = lax.broadcasted_iota(jnp.int32, (T, LANES), 1)
    row = lax.broadcasted_iota(jnp.int32, (T, LANES), 0)
    live = jnp.logical_and(lane < nheads, row < valid)
    xdt = dtr_ref[...] + dtb_ref[...]
    dt = jnp.where(live, jnp.maximum(xdt, 0.0) + jnp.log1p(jnp.exp(-jnp.abs(xdt))), 0.0)
    da = dt * (-jnp.exp(alog_ref[...]))
    tr = lax.broadcasted_iota(jnp.int32, (T, T), 0)
    tc = lax.broadcasted_iota(jnp.int32, (T, T), 1)
    causal = tr >= tc
    ltri = jnp.where(causal, 1.0, 0.0).astype(BF16)
    cs = jnp.dot(jnp.concatenate([ltri, ltri, ltri], axis=1),
                 jnp.concatenate(_split3(da), axis=0), preferred_element_type=F32)
    cs_t = cs.T

    stack = jnp.concatenate([dt, cs], axis=0)
    exp64 = jnp.dot(jnp.concatenate(_split3(stack), axis=1), e64_ref[...], preferred_element_type=F32)
    dt_x = exp64[:T]
    cs_x = exp64[T:]
    cs_b = jnp.dot(jnp.concatenate(_split3(cs), axis=1), et_ref[...], preferred_element_type=F32)

    cs_last = cs_x[T - 1:T, :]
    ecs_x = jnp.exp(cs_x)
    dte_x = jnp.exp(cs_last - cs_x)
    x = xs * dt_x
    xb = x.astype(BF16)
    xdb = (x * dte_x).astype(BF16)

    lane2 = lax.broadcasted_iota(jnp.int32, (T, LANES), 1)
    y_parts = []
    for g in range(SSM_GROUPS):
        bg = bm[:, g * SSM_STATE:(g + 1) * SSM_STATE]
        cg = cm[:, g * SSM_STATE:(g + 1) * SSM_STATE]
        cb = lax.dot_general(cg, bg, (((1,), (1,)), ((), ())), preferred_element_type=F32)
        st_g = st_ref[:, g * gw:(g + 1) * gw]
        y_off = jnp.dot(cg, st_g.astype(BF16), preferred_element_type=F32) * ecs_x[:, g * gw:(g + 1) * gw]
        y_diag = []
        for pr in range(gw // LANES):
            mats = []
            for hh in range(LANES // SSM_HEAD_DIM):
                h = g * hpg + pr * (LANES // SSM_HEAD_DIM) + hh
                seg = cs_b[:, h * T:(h + 1) * T] - cs_t[h:h + 1, :]
                lmat = jnp.exp(jnp.where(causal, seg, -jnp.inf))
                mats.append((cb * lmat).astype(BF16))
            tile = g * (gw // LANES) + pr
            x2 = xb[:, tile * LANES:(tile + 1) * LANES]
            zero = jnp.zeros_like(x2)
            xbd = jnp.concatenate([jnp.where(lane2 < SSM_HEAD_DIM, x2, zero),
                                   jnp.where(lane2 < SSM_HEAD_DIM, zero, x2)], axis=0)
            y_diag.append(jnp.dot(jnp.concatenate(mats, axis=1), xbd, preferred_element_type=F32))
        y_parts.append(jnp.concatenate(y_diag, axis=1) + y_off)
        bg_t = xc[:, width + g * SSM_STATE:width + (g + 1) * SSM_STATE].T.astype(BF16)
        new_states = jnp.dot(bg_t, xdb[:, g * gw:(g + 1) * gw], preferred_element_type=F32)
        decay = ecs_x[T - 1:T, g * gw:(g + 1) * gw]
        st_ref[:, g * gw:(g + 1) * gw] = st_g * decay + new_states

    y = jnp.concatenate(y_parts, axis=1) + dskip_ref[...] * xs
    y = y * _silu(z_ref[...])
    outs = []
    for g in range(SSM_GROUPS):
        yg = y[:, g * gw:(g + 1) * gw]
        ms = jnp.mean(yg * yg, axis=-1, keepdims=True)
        outs.append(yg * lax.rsqrt(ms + NORM_EPS))
    y_ref[...] = (jnp.concatenate(outs, axis=1) * ng_ref[...]).astype(y_ref.dtype)

    @pl.when(c == nc - 1)
    def _():
        for j in range(ntile):
            hout_ref[j * LANES // SSM_HEAD_DIM:(j + 1) * LANES // SSM_HEAD_DIM] = (
                st_ref[:, j * LANES:(j + 1) * LANES].T.reshape(LANES // SSM_HEAD_DIM, SSM_HEAD_DIM, SSM_STATE))


def ssd_mixer(xbc, z, dtr, conv_prev, h0, conv_w, conv_b, dt_bias, a_log, d_skip, norm_g,
              batch, valid):
    mtot, conv_dim = xbc.shape
    width = z.shape[1]
    nheads = width // SSM_HEAD_DIM
    T = SSD_CHUNK
    nc = mtot // batch // T
    hid = jnp.arange(LANES)[:, None]
    e64 = (jnp.arange(width)[None, :] // SSM_HEAD_DIM == hid).astype(BF16)
    et = (jnp.arange(nheads * T)[None, :] // T == hid).astype(BF16)
    e64 = jnp.concatenate([e64] * 3, axis=0)
    et = jnp.concatenate([et] * 3, axis=0)
    pad_l = lambda v: jnp.pad(v.astype(F32), (0, LANES - nheads)).reshape(1, LANES)
    cw = jnp.pad(conv_w, ((0, SUBLANES - CONV_K), (0, 0)))
    full = lambda shape: pl.BlockSpec(shape, lambda b, c: (0,) * len(shape))
    y, hout = pl.pallas_call(
        functools.partial(_ssd_body, valid=valid, nheads=nheads),
        grid=(batch, nc),
        in_specs=[pl.BlockSpec((T, conv_dim), lambda b, c: (b * nc + c, 0)),
                  pl.BlockSpec((T, width), lambda b, c: (b * nc + c, 0)),
                  pl.BlockSpec((T, LANES), lambda b, c: (b * nc + c, 0)),
                  pl.BlockSpec((None, SUBLANES, conv_dim), lambda b, c: (b, 0, 0)),
                  pl.BlockSpec((None, nheads, SSM_HEAD_DIM, SSM_STATE), lambda b, c: (b, 0, 0, 0)),
                  full((SUBLANES, conv_dim)), full((1, conv_dim)), full((1, LANES)), full((1, LANES)),
                  full((1, width)), full((1, width)), full((3 * LANES, width)), full((3 * LANES, nheads * T))],
        out_specs=[pl.BlockSpec((T, width), lambda b, c: (b * nc + c, 0)),
                   pl.BlockSpec((None, nheads, SSM_HEAD_DIM, SSM_STATE), lambda b, c: (b, 0, 0, 0))],
        out_shape=[jax.ShapeDtypeStruct((mtot, width), BF16),
                   jax.ShapeDtypeStruct((batch, nheads, SSM_HEAD_DIM, SSM_STATE), F32)],
        scratch_shapes=[pltpu.VMEM((SUBLANES, conv_dim), F32),
                        pltpu.VMEM((SSM_STATE, width), F32),
                        pltpu.VMEM((SUBLANES + T, conv_dim), F32)],
        compiler_params=_cparams(("parallel", "arbitrary")),
        name="ssd_mixer",
    )(xbc, z, dtr, conv_prev, h0, cw, conv_b.reshape(1, conv_dim), pad_l(dt_bias), pad_l(a_log),
      jnp.repeat(d_skip.astype(F32), SSM_HEAD_DIM).reshape(1, width), norm_g.reshape(1, width), e64, et)
    return y, hout


def _layer(x_p, x_s, l, lam_init, dims, cache_k, cache_v, state_conv, state_ssm, page_table, w, kv_bufs):
    bp, seq, db, dseq = dims
    att_w, ssm_w, conv_dim, nheads = w["att_w"], w["ssm_w"], w["conv_dim"], w["nheads"]
    nh = att_w // HEAD_W
    qscale = D_HEAD ** -0.5
    offs = {"q": 0, "k": att_w, "v": 2 * att_w, "z": 3 * att_w, "xbc": 3 * att_w + ssm_w}

    def in_proj(x):
        u = rmsnorm(x, w["attn_norm_g"][l], BF16)
        mm = functools.partial(matmul, u, w["w_in"], l)
        z = mm(n_off=offs["z"], n=ssm_w)
        xbc = mm(n_off=offs["xbc"], n=conv_dim)
        dtr = matmul(u, w["w_dt"], l)
        return mm, z, xbc, dtr

    def tail(x, o, y):
        h = matmul((o, y), w["w_out"], l, res=x)
        m = rmsnorm(h, w["mlp_norm_g"][l], BF16)
        a = matmul(m, w["w_up"], l, out_dtype=BF16, act="relu2")
        return matmul(a, w["w_down"], l, res=h, tk=2048)

    ssm_args = (w["conv_w"][l], w["conv_b"][l], w["dt_bias"][l], w["a_log"][l], w["d_skip"][l],
                w["ssm_norm_g"][l])

    mm, z, xbc, dtr = in_proj(x_p)
    k_all = mm(n_off=offs["k"], n=att_w, out_buf=kv_bufs[0])
    v_all = mm(n_off=offs["v"], n=att_w, out_buf=kv_bufs[1])
    q = mm(n_off=offs["q"], n=att_w, scale=qscale)
    o = attn_prompt(q, k_all, v_all, l, w["lambda_qk"][l], w["subln_g"][l], lam_init, bp, seq)
    y, ssm_p = ssd_mixer(xbc, z, dtr, jnp.zeros((bp, SUBLANES, conv_dim), F32),
                         jnp.zeros((bp, nheads, SSM_HEAD_DIM, SSM_STATE), F32), *ssm_args,
                         batch=bp, valid=SSD_CHUNK)
    conv_p = xbc.reshape(bp, seq, conv_dim)[:, seq - (CONV_K - 1):]
    x_p = tail(x_p, o, y)

    mm, z, xbc, dtr = in_proj(x_s)
    k_s = mm(n_off=offs["k"], n=att_w)
    v_s = mm(n_off=offs["v"], n=att_w)
    q = mm(n_off=offs["q"], n=att_w)
    q5 = q.reshape(db, dseq, nh, 2, D_HEAD).transpose(0, 2, 1, 3, 4)
    sel = jnp.eye(2, dtype=F32)[None, None, :, None, :, None]
    qrows = (q5[:, :, None] * sel).reshape(db, nh * 8, HEAD_W)

    def tok_rows(t):
        t = t.reshape(db, dseq, nh, HEAD_W)
        return jnp.pad(t, ((0, 0), (0, SUBLANES - dseq), (0, 0), (0, 0))).reshape(db, SUBLANES * nh, HEAD_W)

    o = attn_sample(qrows, cache_k, cache_v, tok_rows(k_s), tok_rows(v_s), page_table,
                    w["lambda_qk"][l], w["subln_g"][l], lam_init, l, dseq)
    o = o.reshape(db, nh, 8, HEAD_W)[:, :, :dseq].transpose(0, 2, 1, 3).reshape(db * dseq, att_w)

    def pad_chunk(t):
        t = t.reshape(db, dseq, t.shape[-1])
        return jnp.pad(t, ((0, 0), (0, SSD_CHUNK - dseq), (0, 0))).reshape(db * SSD_CHUNK, t.shape[-1])

    cprev = jnp.pad(state_conv[l], ((0, 0), (SUBLANES - (CONV_K - 1), 0), (0, 0)))
    y, ssm_s = ssd_mixer(pad_chunk(xbc), pad_chunk(z), pad_chunk(dtr), cprev, state_ssm[l], *ssm_args,
                         batch=db, valid=dseq)
    y = y.reshape(db, SSD_CHUNK, ssm_w)[:, :dseq].reshape(db * dseq, ssm_w)
    conv_s = jnp.concatenate([state_conv[l], xbc.reshape(db, dseq, conv_dim)], axis=1)[:, -(CONV_K - 1):]
    x_s = tail(x_s, o.astype(BF16), y)

    outs = (conv_p, ssm_p, k_s.reshape(db, dseq, nh, HEAD_W), v_s.reshape(db, dseq, nh, HEAD_W), conv_s, ssm_s)
    return x_p, x_s, (k_all, v_all), outs


def kernel(x_prompt, x_sample, cache_k, cache_v, state_conv, state_ssm, page_table, attn_norm_g, w_in,
           lambda_qk, subln_g, conv_w, conv_b, dt_bias, a_log, d_skip, ssm_norm_g, w_out, mlp_norm_g,
           w_up, w_down, final_norm_g):
    bp, seq, d_model = x_prompt.shape
    db, dseq, _ = x_sample.shape
    depth = w_in.shape[0]
    nheads = dt_bias.shape[1]
    ssm_w = ssm_norm_g.shape[1]
    att_w = d_model - ssm_w
    nh = att_w // HEAD_W
    conv_dim = conv_w.shape[2]
    main_cols = 3 * att_w + ssm_w + conv_dim
    w = dict(att_w=att_w, ssm_w=ssm_w, conv_dim=conv_dim, nheads=nheads,
             w_in=w_in.astype(BF16),
             w_dt=jnp.pad(w_in[:, :, main_cols:], ((0, 0), (0, 0), (0, LANES - nheads))).astype(BF16),
             w_out=w_out.astype(BF16), w_up=w_up.astype(BF16), w_down=w_down.astype(BF16),
             attn_norm_g=attn_norm_g, mlp_norm_g=mlp_norm_g, lambda_qk=lambda_qk, subln_g=subln_g,
             conv_w=conv_w, conv_b=conv_b, dt_bias=dt_bias, a_log=a_log, d_skip=d_skip, ssm_norm_g=ssm_norm_g)
    x_p = x_prompt.reshape(bp * seq, d_model)
    x_s = x_sample.reshape(db * dseq, d_model)
    kv_bufs = (jnp.zeros((depth, bp * seq, att_w), F32), jnp.zeros((depth, bp * seq, att_w), F32))
    per_layer = []
    for l in range(depth):
        lam_init = 0.8 - 0.6 * math.exp(-0.3 * l)
        x_p, x_s, kv_bufs, outs = _layer(x_p, x_s, l, lam_init, (bp, seq, db, dseq), cache_k, cache_v,
                                         state_conv, state_ssm, page_table, w, kv_bufs)
        per_layer.append(outs)
    y_p = rmsnorm(x_p, final_norm_g, F32).reshape(bp, seq, d_model)
    y_s = rmsnorm(x_s, final_norm_g, F32).reshape(db, dseq, d_model)
    conv_p, ssm_p, k_s, v_s, conv_s, ssm_s = [jnp.stack([per_layer[l][i] for l in range(depth)])
                                              for i in range(6)]
    k_p = kv_bufs[0].reshape(depth, bp, seq, nh, HEAD_W)
    v_p = kv_bufs[1].reshape(depth, bp, seq, nh, HEAD_W)
    return (y_p, y_s, k_p, v_p, conv_p, ssm_p, k_s, v_s, conv_s, ssm_s)
```

```python
import functools
import math

import jax
import jax.numpy as jnp
from jax import lax
from jax.experimental import pallas as pl
from jax.experimental.pallas import tpu as pltpu

D_HEAD = 64
HEAD_W = 2 * D_HEAD
SSM_HEAD_DIM = 64
SSM_GROUPS = 8
SSM_STATE = 128
CONV_K = 4
SSD_CHUNK = 128
NORM_EPS = 1e-6
LANES = 128
SUBLANES = 8
VMEM_LIMIT = 56 * 1024 * 1024
PAGES_PER_STEP = 4

F32 = jnp.float32
BF16 = jnp.bfloat16


def _cparams(sem):
    return pltpu.CompilerParams(dimension_semantics=sem, vmem_limit_bytes=VMEM_LIMIT)


def _rmsnorm_body(x_ref, g_ref, o_ref):
    x = x_ref[...]
    ms = jnp.mean(x * x, axis=-1, keepdims=True)
    o_ref[...] = (x * lax.rsqrt(ms + NORM_EPS) * g_ref[...]).astype(o_ref.dtype)


def rmsnorm(x, g, out_dtype, rows=256):
    m, d = x.shape
    rows = min(rows, m)
    return pl.pallas_call(
        _rmsnorm_body,
        grid=(m // rows,),
        in_specs=[pl.BlockSpec((rows, d), lambda i: (i, 0)),
                  pl.BlockSpec((1, d), lambda i: (0, 0))],
        out_specs=pl.BlockSpec((rows, d), lambda i: (i, 0)),
        out_shape=jax.ShapeDtypeStruct((m, d), out_dtype),
        compiler_params=_cparams(("parallel",)),
        name="rmsnorm",
    )(x, g.reshape(1, d))


def _mm_body(*refs, n_a, nk, act, scale, has_res, has_buf):
    ap_refs, as_refs, w_ref = refs[:n_a], refs[n_a:2 * n_a], refs[2 * n_a]
    pos = 2 * n_a + 1
    rp_ref = rs_ref = None
    if has_res:
        rp_ref, rs_ref = refs[pos:pos + 2]
        pos += 2
    pos += int(has_buf)
    op_ref, os_ref, wb_ref = refs[pos:pos + 3]
    scr = refs[pos + 3:]
    k, i = pl.program_id(1), pl.program_id(2)

    @pl.when(i == 0)
    def _():
        wb_ref[...] = w_ref[...].astype(BF16)

    def lhs_dot(a_refs):
        off, tot = 0, None
        for a_ref in a_refs:
            kk = a_ref.shape[1]
            d = jnp.dot(a_ref[...], wb_ref[off:off + kk, :], preferred_element_type=F32)
            tot = d if tot is None else tot + d
            off += kk
        return tot

    def finish(acc, r_ref, o_ref):
        if act == "relu2":
            acc = jnp.square(jnp.maximum(acc, 0.0))
        if scale != 1.0:
            acc = acc * scale
        if r_ref is not None:
            acc = acc + r_ref[...]
        o_ref[...] = acc.astype(o_ref.dtype)

    def accumulate(a_refs, acc_ref, r_ref, o_ref):
        part = lhs_dot(a_refs)
        if nk == 1:
            finish(part, r_ref, o_ref)
            return

        @pl.when(k == 0)
        def _():
            acc_ref[...] = part

        @pl.when(jnp.logical_and(k > 0, k < nk - 1))
        def _():
            acc_ref[...] += part

        @pl.when(k == nk - 1)
        def _():
            finish(acc_ref[...] + part, r_ref, o_ref)

    accumulate(ap_refs, scr[0].at[i] if nk > 1 else None, rp_ref, op_ref)

    @pl.when(i == 0)
    def _():
        accumulate(as_refs, scr[1] if nk > 1 else None, rs_ref, os_ref)


def matmul(a_p, a_s, w, layer, *, n_off=0, n=None, out_dtype=F32, act=None, scale=1.0, res=None, out_buf=None,
           tm=1024, tn=512, tk=4096):
    ap_list = a_p if isinstance(a_p, tuple) else (a_p,)
    as_list = a_s if isinstance(a_s, tuple) else (a_s,)
    n_a = len(ap_list)
    mp, ms = ap_list[0].shape[0], as_list[0].shape[0]
    kdim = sum(x.shape[1] for x in ap_list)
    n = w.shape[2] - n_off if n is None else n
    tm, tn = min(tm, mp), min(tn, n)
    tk = kdim if n_a > 1 else min(tk, kdim)
    assert mp % tm == 0 and n % tn == 0 and kdim % tk == 0 and n_off % tn == 0
    ni, nj, nk = mp // tm, n // tn, kdim // tk
    joff = n_off // tn
    row = (lambda i, k: i) if nk == 1 else (lambda i, k: jnp.where(k == nk - 1, i, 0))
    if n_a > 1:
        in_specs = [pl.BlockSpec((tm, x.shape[1]), lambda j, k, i: (i, 0)) for x in ap_list]
        in_specs += [pl.BlockSpec((ms, x.shape[1]), lambda j, k, i: (0, 0)) for x in as_list]
    else:
        in_specs = [pl.BlockSpec((tm, tk), lambda j, k, i: (i, k)),
                    pl.BlockSpec((ms, tk), lambda j, k, i: (0, k))]
    in_specs.append(pl.BlockSpec((None, tk, tn), lambda j, k, i: (layer, k, j + joff)))
    args = list(ap_list) + list(as_list) + [w]
    if res is not None:
        in_specs += [pl.BlockSpec((tm, tn), lambda j, k, i: (row(i, k), j)),
                     pl.BlockSpec((ms, tn), lambda j, k, i: (0, j))]
        args += list(res)
    aliases = {}
    if out_buf is not None:
        in_specs.append(pl.BlockSpec(memory_space=pl.ANY))
        aliases = {len(args): 0}
        args.append(out_buf)
        out_p_spec = pl.BlockSpec((None, tm, tn), lambda j, k, i: (layer, row(i, k), j))
        out_p_shape = jax.ShapeDtypeStruct(out_buf.shape, out_buf.dtype)
    else:
        out_p_spec = pl.BlockSpec((tm, tn), lambda j, k, i: (row(i, k), j))
        out_p_shape = jax.ShapeDtypeStruct((mp, n), out_dtype)
    scratch = [pltpu.VMEM((tk, tn), BF16)]
    if nk > 1:
        scratch += [pltpu.VMEM((ni, tm, tn), F32), pltpu.VMEM((ms, tn), F32)]
    return pl.pallas_call(
        functools.partial(_mm_body, n_a=n_a, nk=nk, act=act, scale=scale,
                          has_res=res is not None, has_buf=out_buf is not None),
        grid=(nj, nk, ni),
        in_specs=in_specs,
        out_specs=[out_p_spec, pl.BlockSpec((ms, tn), lambda j, k, i: (0, j))],
        out_shape=[out_p_shape, jax.ShapeDtypeStruct((ms, n), out_dtype)],
        scratch_shapes=scratch,
        input_output_aliases=aliases,
        compiler_params=_cparams(("parallel", "arbitrary", "arbitrary")),
        name="matmul",
    )(*args)


def _lam_value(lq_ref, lam_init):
    lq = lq_ref[...]
    a = jnp.sum(lq[0:1] * lq[1:2], axis=-1, keepdims=True)
    b = jnp.sum(lq[2:3] * lq[3:4], axis=-1, keepdims=True)
    return jnp.exp(a) - jnp.exp(b) + lam_init


def _head_norm(o, g, lam_init):
    ms = jnp.mean(o * o, axis=-1, keepdims=True)
    return o * lax.rsqrt(ms + NORM_EPS) * g * (1.0 - lam_init)


def _attn_prompt_body(q_ref, k_ref, v_ref, lq_ref, g_ref, o_ref, kb_ref, vt_ref, st_ref, acc_ref, l_ref,
                      *, bq, lam_init):
    seq = q_ref.shape[0]
    nq = seq // bq
    kb_ref[...] = k_ref[...].astype(BF16)
    for j in range(nq):
        vt_ref[j] = v_ref[j * bq:(j + 1) * bq, :].T.astype(BF16)
    lam = _lam_value(lq_ref, lam_init)
    sub = lax.broadcasted_iota(jnp.int32, (HEAD_W, bq), 0)
    key_row = lax.broadcasted_iota(jnp.int32, (bq, 2 * bq), 0)
    q_col = lax.broadcasted_iota(jnp.int32, (bq, 2 * bq), 1)
    keep = key_row <= jnp.where(q_col >= bq, q_col - bq, q_col)

    def fold(x):
        return x.reshape(bq // SUBLANES, SUBLANES, 2 * bq)

    def qblock(i, carry):
        qs = pl.multiple_of(i * bq, bq)
        qt = q_ref[pl.ds(qs, bq), :].T
        zero = jnp.zeros_like(qt)
        q2t = jnp.concatenate([jnp.where(sub < D_HEAD, qt, zero), jnp.where(sub < D_HEAD, zero, qt)],
                              axis=1).astype(BF16)

        def scores(j):
            ks = pl.multiple_of(j * bq, bq)
            return ks, jnp.dot(kb_ref[pl.ds(ks, bq), :], q2t, preferred_element_type=F32)

        def pass_a(j, m_run):
            ks, s = scores(j)
            st_ref[pl.ds(ks, bq), :] = s
            return jnp.maximum(m_run, jnp.max(fold(s), axis=0))

        m_run = lax.fori_loop(0, i, pass_a, jnp.full((SUBLANES, 2 * bq), -jnp.inf, F32))
        _, s = scores(i)
        s = jnp.where(keep, s, -jnp.inf)
        st_ref[pl.ds(qs, bq), :] = s
        m_run = jnp.maximum(m_run, jnp.max(fold(s), axis=0))
        m = jnp.max(m_run, axis=0, keepdims=True)

        acc_ref[...] = jnp.zeros(acc_ref.shape, F32)
        l_ref[...] = jnp.zeros(l_ref.shape, F32)

        def pass_b(j, c):
            ks = pl.multiple_of(j * bq, bq)
            p = jnp.exp(st_ref[pl.ds(ks, bq), :] - m)
            l_ref[...] += jnp.sum(fold(p), axis=0)
            acc_ref[...] += jnp.dot(vt_ref[j], p.astype(BF16), preferred_element_type=F32)
            return c

        lax.fori_loop(0, i + 1, pass_b, 0)
        l = jnp.sum(l_ref[...], axis=0, keepdims=True)
        d = acc_ref[...] * (1.0 / l)
        o = (d[:, :bq] - lam * d[:, bq:]).T
        o_ref[pl.ds(qs, bq), :] = _head_norm(o, g_ref[...], lam_init).astype(o_ref.dtype)
        return carry

    lax.fori_loop(0, nq, qblock, 0)


def attn_prompt(q, k_all, v_all, layer, lambda_qk, subln_g, lam_init, batch, seq, bq=512):
    m, width = q.shape
    nh = width // HEAD_W
    kv_spec = pl.BlockSpec((None, seq, HEAD_W), lambda b, h: (layer, b, h))
    return pl.pallas_call(
        functools.partial(_attn_prompt_body, bq=bq, lam_init=lam_init),
        grid=(batch, nh),
        in_specs=[pl.BlockSpec((seq, HEAD_W), lambda b, h: (b, h)), kv_spec, kv_spec,
                  pl.BlockSpec((4, D_HEAD), lambda b, h: (0, 0)),
                  pl.BlockSpec((1, HEAD_W), lambda b, h: (0, 0))],
        out_specs=pl.BlockSpec((seq, HEAD_W), lambda b, h: (b, h)),
        out_shape=jax.ShapeDtypeStruct((m, width), BF16),
        scratch_shapes=[pltpu.VMEM((seq, HEAD_W), BF16),
                        pltpu.VMEM((seq // bq, HEAD_W, bq), BF16),
                        pltpu.VMEM((seq, 2 * bq), F32),
                        pltpu.VMEM((HEAD_W, 2 * bq), F32),
                        pltpu.VMEM((SUBLANES, 2 * bq), F32)],
        compiler_params=_cparams(("parallel", "parallel")),
        name="attn_prompt",
    )(q, k_all, v_all, lambda_qk, subln_g.reshape(1, HEAD_W))


def _attn_sample_body(pt_ref, q_ref, bias_ref, *refs, n_steps, npp, nh, n_new, lam_init):
    kc_refs, vc_refs = refs[:npp], refs[npp:2 * npp]
    kn_ref, vn_ref, lq_ref, g_ref, o_ref, m_ref, l_ref, acc_ref = refs[2 * npp:]
    p = pl.program_id(1)
    rows = q_ref.shape[0]
    q = q_ref[...].astype(BF16)

    @pl.when(p == 0)
    def _():
        m_ref[...] = jnp.full(m_ref.shape, -jnp.inf, F32)
        l_ref[...] = jnp.zeros(l_ref.shape, F32)
        acc_ref[...] = jnp.zeros(acc_ref.shape, F32)

    def update(kv_pairs, biases):
        s_list = [lax.dot_general(q, kf.astype(BF16), (((1,), (1,)), ((), ())), preferred_element_type=F32) + b
                  for (kf, _), b in zip(kv_pairs, biases)]
        m_old = m_ref[...]
        s_max = functools.reduce(jnp.maximum, [jnp.max(s, axis=-1, keepdims=True) for s in s_list])
        m_new = jnp.maximum(m_old, s_max)
        alpha = jnp.exp(m_old - m_new)
        l_new = alpha * l_ref[...]
        acc = alpha * acc_ref[...]
        for s, (_, vf) in zip(s_list, kv_pairs):
            pr = jnp.exp(s - jnp.concatenate([m_new] * (s.shape[1] // LANES), axis=1))
            l_new = l_new + jnp.sum(pr, axis=-1, keepdims=True)
            acc = acc + jnp.dot(pr.astype(BF16), vf.astype(BF16), preferred_element_type=F32)
        m_ref[...] = m_new
        l_ref[...] = l_new
        acc_ref[...] = acc

    @pl.when(p < n_steps)
    def _():
        cols = kc_refs[0].shape[0] * nh
        bias = bias_ref[...]
        update([(kr[...].reshape(cols, HEAD_W), vr[...].reshape(cols, HEAD_W))
                for kr, vr in zip(kc_refs, vc_refs)], [bias] * npp)

    @pl.when(p == n_steps)
    def _():
        cols = kn_ref.shape[0]
        r = lax.broadcasted_iota(jnp.int32, (rows, cols), 0)
        c = lax.broadcasted_iota(jnp.int32, (rows, cols), 1)
        keep = jnp.logical_and((c % nh) == (r // 8), (c // nh) <= (r % n_new))
        update([(kn_ref[...], vn_ref[...])], [jnp.where(keep, 0.0, -jnp.inf)])
        lam = _lam_value(lq_ref, lam_init)
        d = acc_ref[...] / l_ref[...]
        o = d - lam * pltpu.roll(d, rows - n_new, axis=0)
        o_ref[...] = _head_norm(o, g_ref[...], lam_init).astype(o_ref.dtype)


def attn_sample(qrows, cache_k, cache_v, knew, vnew, page_table, lambda_qk, subln_g, lam_init, layer, n_new):
    nb, rows, _ = qrows.shape
    nh = rows // 8
    assert 2 * n_new == 8
    n_pages = page_table.shape[1]
    page = cache_k.shape[2]
    npp = PAGES_PER_STEP
    assert n_pages % npp == 0
    n_steps = n_pages // npp
    cols = page * nh
    bias = jnp.where((jnp.arange(cols)[None, :] % nh) == (jnp.arange(rows)[:, None] // 8), 0.0, -jnp.inf).astype(F32)

    def cache_spec(i):
        def cache_map(b, p, pt):
            return (layer, pt[b, jnp.minimum(p, n_steps - 1) * npp + i], 0, 0, 0)
        return pl.BlockSpec((None, None, page, nh, HEAD_W), cache_map)

    per_b = lambda shape: pl.BlockSpec((None,) + shape, lambda b, p, pt: (b, 0, 0))
    const = lambda shape: pl.BlockSpec(shape, lambda b, p, pt: (0, 0))
    grid_spec = pltpu.PrefetchScalarGridSpec(
        num_scalar_prefetch=1,
        grid=(nb, n_steps + 1),
        in_specs=[per_b((rows, HEAD_W)), const((rows, cols))]
                 + [cache_spec(i) for i in range(npp)] * 2
                 + [per_b((8 * nh, HEAD_W)), per_b((8 * nh, HEAD_W)), const((4, D_HEAD)), const((1, HEAD_W))],
        out_specs=per_b((rows, HEAD_W)),
        scratch_shapes=[pltpu.VMEM((rows, LANES), F32), pltpu.VMEM((rows, LANES), F32),
                        pltpu.VMEM((rows, HEAD_W), F32)],
    )
    return pl.pallas_call(
        functools.partial(_attn_sample_body, n_steps=n_steps, npp=npp, nh=nh, n_new=n_new, lam_init=lam_init),
        grid_spec=grid_spec,
        out_shape=jax.ShapeDtypeStruct((nb, rows, HEAD_W), F32),
        compiler_params=_cparams(("parallel", "arbitrary")),
        name="attn_sample",
    )(page_table, qrows, bias, *([cache_k] * npp), *([cache_v] * npp), knew, vnew, lambda_qk,
      subln_g.reshape(1, HEAD_W))


def _split3(x):
    hi = x.astype(BF16)
    r1 = x - hi.astype(F32)
    mid = r1.astype(BF16)
    lo = (r1 - mid.astype(F32)).astype(BF16)
    return hi, mid, lo


def _silu(x):
    return x * (1.0 / (1.0 + jnp.exp(-x)))


def _ssd_body(xbc_ref, z_ref, dtr_ref, cprev_ref, h0_ref, cw_ref, cb_ref, dtb_ref, alog_ref,
              dskip_ref, ng_ref, e64_ref, et_ref, y_ref, hout_ref,
              tail_ref, st_ref, xpad_ref, *, valid, nheads):
    c = pl.program_id(1)
    nc = pl.num_programs(1)
    T = xbc_ref.shape[0]
    width = z_ref.shape[1]
    gn = SSM_GROUPS * SSM_STATE
    hpg = nheads // SSM_GROUPS
    gw = hpg * SSM_HEAD_DIM
    ntile = width // LANES

    @pl.when(c == 0)
    def _():
        tail_ref[...] = cprev_ref[...]
        h0 = h0_ref[...].reshape(width, SSM_STATE)
        for j in range(ntile):
            st_ref[:, j * LANES:(j + 1) * LANES] = h0[j * LANES:(j + 1) * LANES, :].T

    xbc = xbc_ref[...]
    xpad_ref[0:SUBLANES, :] = tail_ref[...]
    xpad_ref[SUBLANES:SUBLANES + T, :] = xbc
    tail_ref[...] = xbc[T - SUBLANES:, :]
    acc = cb_ref[...] + xbc * cw_ref[CONV_K - 1:CONV_K, :]
    for j in range(CONV_K - 1):
        off = SUBLANES - (CONV_K - 1) + j
        acc = acc + xpad_ref[off:off + T, :] * cw_ref[j:j + 1, :]
    xc = _silu(acc)
    xs = xc[:, :width]
    bm = xc[:, width:width + gn].astype(BF16)
    cm = xc[:, width + gn:].astype(BF16)

    lane = lax.broadcasted_iota(jnp.int32, (T, LANES), 1)
    row = lax.broadcasted_iota(jnp.int32, (T, LANES), 0)
    live = jnp.logical_and(lane < nheads, row < valid)
    xdt = dtr_ref[...] + dtb_ref[...]
    dt = jnp.where(live, jnp.maximum(xdt, 0.0) + jnp.log1p(jnp.exp(-jnp.abs(xdt))), 0.0)
    da = dt * (-jnp.exp(alog_ref[...]))
    tr = lax.broadcasted_iota(jnp.int32, (T, T), 0)
    tc = lax.broadcasted_iota(jnp.int32, (T, T), 1)
    causal = tr >= tc
    ltri = jnp.where(causal, 1.0, 0.0).astype(BF16)
    cs = jnp.dot(jnp.concatenate([ltri, ltri, ltri], axis=1),
                 jnp.concatenate(_split3(da), axis=0), preferred_element_type=F32)
    cs_t = cs.T

    stack = jnp.concatenate([dt, cs], axis=0)
    exp64 = jnp.dot(jnp.concatenate(_split3(stack), axis=1), e64_ref[...], preferred_element_type=F32)
    dt_x = exp64[:T]
    cs_x = exp64[T:]
    cs_b = jnp.dot(jnp.concatenate(_split3(cs), axis=1), et_ref[...], preferred_element_type=F32)

    cs_last = cs_x[T - 1:T, :]
    ecs_x = jnp.exp(cs_x)
    dte_x = jnp.exp(cs_last - cs_x)
    x = xs * dt_x
    xb = x.astype(BF16)
    xdb = (x * dte_x).astype(BF16)

    lane2 = lax.broadcasted_iota(jnp.int32, (T, LANES), 1)
    y_parts = []
    for g in range(SSM_GROUPS):
        bg = bm[:, g * SSM_STATE:(g + 1) * SSM_STATE]
        cg = cm[:, g * SSM_STATE:(g + 1) * SSM_STATE]
        cb = lax.dot_general(cg, bg, (((1,), (1,)), ((), ())), preferred_element_type=F32)
        st_g = st_ref[:, g * gw:(g + 1) * gw]
        y_off = jnp.dot(cg, st_g.astype(BF16), preferred_element_type=F32) * ecs_x[:, g * gw:(g + 1) * gw]
        y_diag = []
        for pr in range(gw // LANES):
            mats = []
            for hh in range(LANES // SSM_HEAD_DIM):
                h = g * hpg + pr * (LANES // SSM_HEAD_DIM) + hh
                seg = cs_b[:, h * T:(h + 1) * T] - cs_t[h:h + 1, :]
                lmat = jnp.exp(jnp.where(causal, seg, -jnp.inf))
                mats.append((cb * lmat).astype(BF16))
            tile = g * (gw // LANES) + pr
            x2 = xb[:, tile * LANES:(tile + 1) * LANES]
            zero = jnp.zeros_like(x2)
            xbd = jnp.concatenate([jnp.where(lane2 < SSM_HEAD_DIM, x2, zero),
                                   jnp.where(lane2 < SSM_HEAD_DIM, zero, x2)], axis=0)
            y_diag.append(jnp.dot(jnp.concatenate(mats, axis=1), xbd, preferred_element_type=F32))
        y_parts.append(jnp.concatenate(y_diag, axis=1) + y_off)
        bg_t = xc[:, width + g * SSM_STATE:width + (g + 1) * SSM_STATE].T.astype(BF16)
        new_states = jnp.dot(bg_t, xdb[:, g * gw:(g + 1) * gw], preferred_element_type=F32)
        decay = ecs_x[T - 1:T, g * gw:(g + 1) * gw]
        st_ref[:, g * gw:(g + 1) * gw] = st_g * decay + new_states

    y = jnp.concatenate(y_parts, axis=1) + dskip_ref[...] * xs
    y = y * _silu(z_ref[...])
    outs = []
    for g in range(SSM_GROUPS):
        yg = y[:, g * gw:(g + 1) * gw]
        ms = jnp.mean(yg * yg, axis=-1, keepdims=True)
        outs.append(yg * lax.rsqrt(ms + NORM_EPS))
    y_ref[...] = (jnp.concatenate(outs, axis=1) * ng_ref[...]).astype(y_ref.dtype)

    @pl.when(c == nc - 1)
    def _():
        for j in range(ntile):
            hout_ref[j * LANES // SSM_HEAD_DIM:(j + 1) * LANES // SSM_HEAD_DIM] = (
                st_ref[:, j * LANES:(j + 1) * LANES].T.reshape(LANES // SSM_HEAD_DIM, SSM_HEAD_DIM, SSM_STATE))


def ssd_mixer(xbc, z, dtr, conv_prev, h0, conv_w, conv_b, dt_bias, a_log, d_skip, norm_g,
              batch, valid):
    mtot, conv_dim = xbc.shape
    width = z.shape[1]
    nheads = width // SSM_HEAD_DIM
    T = SSD_CHUNK
    nc = mtot // batch // T
    hid = jnp.arange(LANES)[:, None]
    e64 = (jnp.arange(width)[None, :] // SSM_HEAD_DIM == hid).astype(BF16)
    et = (jnp.arange(nheads * T)[None, :] // T == hid).astype(BF16)
    e64 = jnp.concatenate([e64] * 3, axis=0)
    et = jnp.concatenate([et] * 3, axis=0)
    pad_l = lambda v: jnp.pad(v.astype(F32), (0, LANES - nheads)).reshape(1, LANES)
    cw = jnp.pad(conv_w, ((0, SUBLANES - CONV_K), (0, 0)))
    full = lambda shape: pl.BlockSpec(shape, lambda b, c: (0,) * len(shape))
    y, hout = pl.pallas_call(
        functools.partial(_ssd_body, valid=valid, nheads=nheads),
        grid=(batch, nc),
        in_specs=[pl.BlockSpec((T, conv_dim), lambda b, c: (b * nc + c, 0)),
                  pl.BlockSpec((T, width), lambda b, c: (b * nc + c, 0)),
                  pl.BlockSpec((T, LANES), lambda b, c: (b * nc + c, 0)),
                  pl.BlockSpec((None, SUBLANES, conv_dim), lambda b, c: (b, 0, 0)),
                  pl.BlockSpec((None, nheads, SSM_HEAD_DIM, SSM_STATE), lambda b, c: (b, 0, 0, 0)),
                  full((SUBLANES, conv_dim)), full((1, conv_dim)), full((1, LANES)), full((1, LANES)),
                  full((1, width)), full((1, width)), full((3 * LANES, width)), full((3 * LANES, nheads * T))],
        out_specs=[pl.BlockSpec((T, width), lambda b, c: (b * nc + c, 0)),
                   pl.BlockSpec((None, nheads, SSM_HEAD_DIM, SSM_STATE), lambda b, c: (b, 0, 0, 0))],
        out_shape=[jax.ShapeDtypeStruct((mtot, width), BF16),
                   jax.ShapeDtypeStruct((batch, nheads, SSM_HEAD_DIM, SSM_STATE), F32)],
        scratch_shapes=[pltpu.VMEM((SUBLANES, conv_dim), F32),
                        pltpu.VMEM((SSM_STATE, width), F32),
                        pltpu.VMEM((SUBLANES + T, conv_dim), F32)],
        compiler_params=_cparams(("parallel", "arbitrary")),
        name="ssd_mixer",
    )(xbc, z, dtr, conv_prev, h0, cw, conv_b.reshape(1, conv_dim), pad_l(dt_bias), pad_l(a_log),
      jnp.repeat(d_skip.astype(F32), SSM_HEAD_DIM).reshape(1, width), norm_g.reshape(1, width), e64, et)
    return y, hout


def _layer(x_p, x_s, l, lam_init, dims, cache_k, cache_v, state_conv, state_ssm, page_table, w, kv_bufs):
    bp, seq, db, dseq = dims
    att_w, ssm_w, conv_dim, nheads = w["att_w"], w["ssm_w"], w["conv_dim"], w["nheads"]
    nh = att_w // HEAD_W
    offs = {"q": 0, "k": att_w, "v": 2 * att_w, "z": 3 * att_w, "xbc": 3 * att_w + ssm_w}
    ssm_args = (w["conv_w"][l], w["conv_b"][l], w["dt_bias"][l], w["a_log"][l], w["d_skip"][l],
                w["ssm_norm_g"][l])

    u_p = rmsnorm(x_p, w["attn_norm_g"][l], BF16)
    u_s = rmsnorm(x_s, w["attn_norm_g"][l], BF16)
    mm = functools.partial(matmul, u_p, u_s, w["w_in"], l)
    q_p, q_s = mm(n_off=offs["q"], n=att_w, scale=D_HEAD ** -0.5)
    k_all, k_s = mm(n_off=offs["k"], n=att_w, out_buf=kv_bufs[0])
    v_all, v_s = mm(n_off=offs["v"], n=att_w, out_buf=kv_bufs[1])
    z_p, z_s = mm(n_off=offs["z"], n=ssm_w)
    xbc_p, xbc_s = mm(n_off=offs["xbc"], n=conv_dim)
    dtr_p, dtr_s = matmul(u_p, u_s, w["w_dt"], l)

    o_p = attn_prompt(q_p, k_all, v_all, l, w["lambda_qk"][l], w["subln_g"][l], lam_init, bp, seq)
    y_p, ssm_p = ssd_mixer(xbc_p, z_p, dtr_p, jnp.zeros((bp, SUBLANES, conv_dim), F32),
                           jnp.zeros((bp, nheads, SSM_HEAD_DIM, SSM_STATE), F32), *ssm_args,
                           batch=bp, valid=SSD_CHUNK)
    conv_p = xbc_p.reshape(bp, seq, conv_dim)[:, seq - (CONV_K - 1):]

    q5 = q_s.reshape(db, dseq, nh, 2, D_HEAD).transpose(0, 2, 1, 3, 4)
    sel = jnp.eye(2, dtype=F32)[None, None, :, None, :, None]
    qrows = (q5[:, :, None] * sel).reshape(db, nh * 8, HEAD_W)

    def tok_rows(t):
        t = t.reshape(db, dseq, nh, HEAD_W)
        return jnp.pad(t, ((0, 0), (0, SUBLANES - dseq), (0, 0), (0, 0))).reshape(db, SUBLANES * nh, HEAD_W)

    o_s = attn_sample(qrows, cache_k, cache_v, tok_rows(k_s), tok_rows(v_s), page_table,
                      w["lambda_qk"][l], w["subln_g"][l], lam_init, l, dseq)
    o_s = o_s.reshape(db, nh, 8, HEAD_W)[:, :, :dseq].transpose(0, 2, 1, 3).reshape(db * dseq, att_w)

    def pad_chunk(t):
        t = t.reshape(db, dseq, t.shape[-1])
        return jnp.pad(t, ((0, 0), (0, SSD_CHUNK - dseq), (0, 0))).reshape(db * SSD_CHUNK, t.shape[-1])

    cprev = jnp.pad(state_conv[l], ((0, 0), (SUBLANES - (CONV_K - 1), 0), (0, 0)))
    y_s, ssm_s = ssd_mixer(pad_chunk(xbc_s), pad_chunk(z_s), pad_chunk(dtr_s), cprev, state_ssm[l], *ssm_args,
                           batch=db, valid=dseq)
    y_s = y_s.reshape(db, SSD_CHUNK, ssm_w)[:, :dseq].reshape(db * dseq, ssm_w)
    conv_s = jnp.concatenate([state_conv[l], xbc_s.reshape(db, dseq, conv_dim)], axis=1)[:, -(CONV_K - 1):]

    h_p, h_s = matmul((o_p, y_p), (o_s.astype(BF16), y_s), w["w_out"], l, res=(x_p, x_s))
    m_p = rmsnorm(h_p, w["mlp_norm_g"][l], BF16)
    m_s = rmsnorm(h_s, w["mlp_norm_g"][l], BF16)
    a_p, a_s = matmul(m_p, m_s, w["w_up"], l, out_dtype=BF16, act="relu2")
    x_p, x_s = matmul(a_p, a_s, w["w_down"], l, res=(h_p, h_s), tk=2048)

    outs = (conv_p, ssm_p, k_s.reshape(db, dseq, nh, HEAD_W), v_s.reshape(db, dseq, nh, HEAD_W), conv_s, ssm_s)
    return x_p, x_s, (k_all, v_all), outs


def kernel(x_prompt, x_sample, cache_k, cache_v, state_conv, state_ssm, page_table, attn_norm_g, w_in,
           lambda_qk, subln_g, conv_w, conv_b, dt_bias, a_log, d_skip, ssm_norm_g, w_out, mlp_norm_g,
           w_up, w_down, final_norm_g):
    bp, seq, d_model = x_prompt.shape
    db, dseq, _ = x_sample.shape
    depth = w_in.shape[0]
    nheads = dt_bias.shape[1]
    ssm_w = ssm_norm_g.shape[1]
    att_w = d_model - ssm_w
    nh = att_w // HEAD_W
    conv_dim = conv_w.shape[2]
    main_cols = 3 * att_w + ssm_w + conv_dim
    w = dict(att_w=att_w, ssm_w=ssm_w, conv_dim=conv_dim, nheads=nheads,
             w_in=w_in, w_dt=jnp.pad(w_in[:, :, main_cols:], ((0, 0), (0, 0), (0, LANES - nheads))),
             w_out=w_out, w_up=w_up, w_down=w_down,
             attn_norm_g=attn_norm_g, mlp_norm_g=mlp_norm_g, lambda_qk=lambda_qk, subln_g=subln_g,
             conv_w=conv_w, conv_b=conv_b, dt_bias=dt_bias, a_log=a_log, d_skip=d_skip, ssm_norm_g=ssm_norm_g)
    x_p = x_prompt.reshape(bp * seq, d_model)
    x_s = x_sample.reshape(db * dseq, d_model)
    kv_bufs = (jnp.zeros((depth, bp * seq, att_w), F32), jnp.zeros((depth, bp * seq, att_w), F32))
    per_layer = []
    for l in range(depth):
        lam_init = 0.8 - 0.6 * math.exp(-0.3 * l)
        x_p, x_s, kv_bufs, outs = _layer(x_p, x_s, l, lam_init, (bp, seq, db, dseq), cache_k, cache_v,
                                         state_conv, state_ssm, page_table, w, kv_bufs)
        per_layer.append(outs)
    y_p = rmsnorm(x_p, final_norm_g, F32).reshape(bp, seq, d_model)
    y_s = rmsnorm(x_s, final_norm_g, F32).reshape(db, dseq, d_model)
    conv_p, ssm_p, k_s, v_s, conv_s, ssm_s = [jnp.stack([per_layer[l][i] for l in range(depth)])
                                              for i in range(6)]
    k_p = kv_bufs[0].reshape(depth, bp, seq, nh, HEAD_W)
    v_p = kv_bufs[1].reshape(depth, bp, seq, nh, HEAD_W)
    return (y_p, y_s, k_p, v_p, conv_p, ssm_p, k_s, v_s, conv_s, ssm_s)
```

```python
import functools
import math

import jax
import jax.numpy as jnp
from jax import lax
from jax.experimental import pallas as pl
from jax.experimental.pallas import tpu as pltpu

D_HEAD = 64
HEAD_W = 2 * D_HEAD
SSM_HEAD_DIM = 64
SSM_GROUPS = 8
SSM_STATE = 128
CONV_K = 4
SSD_CHUNK = 128
NORM_EPS = 1e-6
LANES = 128
SUBLANES = 8
VMEM_LIMIT = 56 * 1024 * 1024
PAGES_PER_STEP = 4

F32 = jnp.float32
BF16 = jnp.bfloat16


def _cparams(sem):
    return pltpu.CompilerParams(dimension_semantics=sem, vmem_limit_bytes=VMEM_LIMIT)


def _rmsnorm_body(x_ref, g_ref, o_ref):
    x = x_ref[...]
    ms = jnp.mean(x * x, axis=-1, keepdims=True)
    o_ref[...] = (x * lax.rsqrt(ms + NORM_EPS) * g_ref[...]).astype(o_ref.dtype)


def rmsnorm(x, g, out_dtype, rows=256):
    m, d = x.shape
    rows = min(rows, m)
    return pl.pallas_call(
        _rmsnorm_body,
        grid=(m // rows,),
        in_specs=[pl.BlockSpec((rows, d), lambda i: (i, 0)),
                  pl.BlockSpec((1, d), lambda i: (0, 0))],
        out_specs=pl.BlockSpec((rows, d), lambda i: (i, 0)),
        out_shape=jax.ShapeDtypeStruct((m, d), out_dtype),
        compiler_params=_cparams(("parallel",)),
        name="rmsnorm",
    )(x, g.reshape(1, d))


def _mm_body(*refs, n_a, nk, act, scale, has_res, has_buf, w_t):
    ap_refs, as_refs, w_ref = refs[:n_a], refs[n_a:2 * n_a], refs[2 * n_a]
    pos = 2 * n_a + 1
    rp_ref = rs_ref = None
    if has_res:
        rp_ref, rs_ref = refs[pos:pos + 2]
        pos += 2
    pos += int(has_buf)
    op_ref, os_ref, wb_ref = refs[pos:pos + 3]
    scr = refs[pos + 3:]
    k, i = pl.program_id(1), pl.program_id(2)

    @pl.when(i == 0)
    def _():
        if w_t:
            for c in range(w_ref.shape[0] // LANES):
                wb_ref[:, c * LANES:(c + 1) * LANES] = w_ref[c * LANES:(c + 1) * LANES, :].T.astype(BF16)
        else:
            wb_ref[...] = w_ref[...].astype(BF16)

    def lhs_dot(a_refs):
        off, tot = 0, None
        for a_ref in a_refs:
            kk = a_ref.shape[1]
            d = jnp.dot(a_ref[...], wb_ref[off:off + kk, :], preferred_element_type=F32)
            tot = d if tot is None else tot + d
            off += kk
        return tot

    def finish(acc, r_ref, o_ref):
        if act == "relu2":
            acc = jnp.square(jnp.maximum(acc, 0.0))
        if scale != 1.0:
            acc = acc * scale
        if r_ref is not None:
            acc = acc + r_ref[...]
        o_ref[...] = acc.astype(o_ref.dtype)

    def accumulate(a_refs, acc_ref, r_ref, o_ref):
        if nk == 1:
            finish(lhs_dot(a_refs), r_ref, o_ref)
            return

        @pl.when(k == 0)
        def _():
            acc_ref[...] = lhs_dot(a_refs)

        @pl.when(jnp.logical_and(k > 0, k < nk - 1))
        def _():
            acc_ref[...] = acc_ref[...] + lhs_dot(a_refs)

        @pl.when(k == nk - 1)
        def _():
            finish(acc_ref[...] + lhs_dot(a_refs), r_ref, o_ref)

    accumulate(ap_refs, scr[0].at[i] if nk > 1 else None, rp_ref, op_ref)

    @pl.when(i == 0)
    def _():
        accumulate(as_refs, scr[1] if nk > 1 else None, rs_ref, os_ref)


def matmul(a_p, a_s, w, layer, *, n_off=0, n=None, out_dtype=F32, act=None, scale=1.0, res=None, out_buf=None,
           stacked=False, w_t=False, tm=1024, tn=512, tk=4096):
    ap_list = a_p if isinstance(a_p, tuple) else (a_p,)
    as_list = a_s if isinstance(a_s, tuple) else (a_s,)
    n_a = len(ap_list)
    mp, ms = ap_list[0].shape[0], as_list[0].shape[0]
    kdim = sum(x.shape[1] for x in ap_list)
    n = w.shape[1 if w_t else 2] - n_off if n is None else n
    tm, tn = min(tm, mp), min(tn, n)
    tk = kdim if n_a > 1 else min(tk, kdim)
    assert mp % tm == 0 and n % tn == 0 and kdim % tk == 0 and n_off % tn == 0
    ni, nj, nk = mp // tm, n // tn, kdim // tk
    joff = n_off // tn
    row = (lambda i, k: i) if nk == 1 else (lambda i, k: jnp.where(k == nk - 1, i, 0))
    if n_a > 1:
        in_specs = [pl.BlockSpec((tm, x.shape[1]), lambda j, k, i: (i, 0)) for x in ap_list]
        in_specs += [pl.BlockSpec((ms, x.shape[1]), lambda j, k, i: (0, 0)) for x in as_list]
    else:
        in_specs = [pl.BlockSpec((tm, tk), lambda j, k, i: (i, k)),
                    pl.BlockSpec((ms, tk), lambda j, k, i: (0, k))]
    if w_t:
        in_specs.append(pl.BlockSpec((None, tn, tk), lambda j, k, i: (layer, j + joff, k)))
    else:
        in_specs.append(pl.BlockSpec((None, tk, tn), lambda j, k, i: (layer, k, j + joff)))
    args = list(ap_list) + list(as_list) + [w]
    if res is not None:
        in_specs += [pl.BlockSpec((tm, tn), lambda j, k, i: (row(i, k), j)),
                     pl.BlockSpec((ms, tn), lambda j, k, i: (0, j))]
        args += list(res)
    aliases = {}
    if stacked:
        if out_buf is not None:
            in_specs.append(pl.BlockSpec(memory_space=pl.ANY))
            aliases = {len(args): 0}
            args.append(out_buf)
        out_p_spec = pl.BlockSpec((None, tm, tn), lambda j, k, i: (layer, row(i, k), j))
        out_p_shape = jax.ShapeDtypeStruct((w.shape[0], mp, n), out_dtype)
    else:
        out_p_spec = pl.BlockSpec((tm, tn), lambda j, k, i: (row(i, k), j))
        out_p_shape = jax.ShapeDtypeStruct((mp, n), out_dtype)
    scratch = [pltpu.VMEM((tk, tn), BF16)]
    if nk > 1:
        scratch += [pltpu.VMEM((ni, tm, tn), F32), pltpu.VMEM((ms, tn), F32)]
    return pl.pallas_call(
        functools.partial(_mm_body, n_a=n_a, nk=nk, act=act, scale=scale,
                          has_res=res is not None, has_buf=out_buf is not None, w_t=w_t),
        grid=(nj, nk, ni),
        in_specs=in_specs,
        out_specs=[out_p_spec, pl.BlockSpec((ms, tn), lambda j, k, i: (0, j))],
        out_shape=[out_p_shape, jax.ShapeDtypeStruct((ms, n), out_dtype)],
        scratch_shapes=scratch,
        input_output_aliases=aliases,
        compiler_params=_cparams(("parallel", "arbitrary", "arbitrary")),
        name="matmul",
    )(*args)


def _lam_value(lq_ref, lam_init):
    lq = lq_ref[...]
    a = jnp.sum(lq[0:1] * lq[1:2], axis=-1, keepdims=True)
    b = jnp.sum(lq[2:3] * lq[3:4], axis=-1, keepdims=True)
    return jnp.exp(a) - jnp.exp(b) + lam_init


def _head_norm(o, g, lam_init):
    ms = jnp.mean(o * o, axis=-1, keepdims=True)
    return o * lax.rsqrt(ms + NORM_EPS) * g * (1.0 - lam_init)


def _attn_prompt_body(q_ref, k_ref, v_ref, lq_ref, g_ref, o_ref, kb_ref, vt_ref, st_ref, acc_ref, l_ref,
                      *, bq, lam_init):
    seq = q_ref.shape[0]
    nq = seq // bq
    kb_ref[...] = k_ref[...].astype(BF16)
    for j in range(nq):
        vt_ref[j] = v_ref[j * bq:(j + 1) * bq, :].T.astype(BF16)
    lam = _lam_value(lq_ref, lam_init)
    sub = lax.broadcasted_iota(jnp.int32, (HEAD_W, bq), 0)
    key_row = lax.broadcasted_iota(jnp.int32, (bq, 2 * bq), 0)
    q_col = lax.broadcasted_iota(jnp.int32, (bq, 2 * bq), 1)
    keep = key_row <= jnp.where(q_col >= bq, q_col - bq, q_col)

    def fold(x):
        return x.reshape(bq // SUBLANES, SUBLANES, 2 * bq)

    def qblock(i, carry):
        qs = pl.multiple_of(i * bq, bq)
        qt = q_ref[pl.ds(qs, bq), :].T
        zero = jnp.zeros_like(qt)
        q2t = jnp.concatenate([jnp.where(sub < D_HEAD, qt, zero), jnp.where(sub < D_HEAD, zero, qt)],
                              axis=1).astype(BF16)

        def scores(j):
            ks = pl.multiple_of(j * bq, bq)
            return ks, jnp.dot(kb_ref[pl.ds(ks, bq), :], q2t, preferred_element_type=F32)

        def pass_a(j, m_run):
            ks, s = scores(j)
            st_ref[pl.ds(ks, bq), :] = s
            return jnp.maximum(m_run, jnp.max(fold(s), axis=0))

        m_run = lax.fori_loop(0, i, pass_a, jnp.full((SUBLANES, 2 * bq), -jnp.inf, F32))
        _, s = scores(i)
        s = jnp.where(keep, s, -jnp.inf)
        st_ref[pl.ds(qs, bq), :] = s
        m_run = jnp.maximum(m_run, jnp.max(fold(s), axis=0))
        m = jnp.max(m_run, axis=0, keepdims=True)

        acc_ref[...] = jnp.zeros(acc_ref.shape, F32)
        l_ref[...] = jnp.zeros(l_ref.shape, F32)

        def pass_b(j, c):
            ks = pl.multiple_of(j * bq, bq)
            p = jnp.exp(st_ref[pl.ds(ks, bq), :] - m)
            l_ref[...] += jnp.sum(fold(p), axis=0)
            acc_ref[...] += jnp.dot(vt_ref[j], p.astype(BF16), preferred_element_type=F32)
            return c

        lax.fori_loop(0, i + 1, pass_b, 0)
        l = jnp.sum(l_ref[...], axis=0, keepdims=True)
        d = acc_ref[...] * (1.0 / l)
        o = (d[:, :bq] - lam * d[:, bq:]).T
        o_ref[pl.ds(qs, bq), :] = _head_norm(o, g_ref[...], lam_init).astype(o_ref.dtype)
        return carry

    lax.fori_loop(0, nq, qblock, 0)


def attn_prompt(q, k_all, v_all, layer, lambda_qk, subln_g, lam_init, batch, seq, bq=512):
    m, width = q.shape
    nh = width // HEAD_W
    kv_spec = pl.BlockSpec((None, seq, HEAD_W), lambda b, h: (layer, b, h))
    return pl.pallas_call(
        functools.partial(_attn_prompt_body, bq=bq, lam_init=lam_init),
        grid=(batch, nh),
        in_specs=[pl.BlockSpec((seq, HEAD_W), lambda b, h: (b, h)), kv_spec, kv_spec,
                  pl.BlockSpec((4, D_HEAD), lambda b, h: (0, 0)),
                  pl.BlockSpec((1, HEAD_W), lambda b, h: (0, 0))],
        out_specs=pl.BlockSpec((seq, HEAD_W), lambda b, h: (b, h)),
        out_shape=jax.ShapeDtypeStruct((m, width), BF16),
        scratch_shapes=[pltpu.VMEM((seq, HEAD_W), BF16),
                        pltpu.VMEM((seq // bq, HEAD_W, bq), BF16),
                        pltpu.VMEM((seq, 2 * bq), F32),
                        pltpu.VMEM((HEAD_W, 2 * bq), F32),
                        pltpu.VMEM((SUBLANES, 2 * bq), F32)],
        compiler_params=_cparams(("parallel", "parallel")),
        name="attn_prompt",
    )(q, k_all, v_all, lambda_qk, subln_g.reshape(1, HEAD_W))


def _attn_sample_body(pt_ref, q_ref, bias_ref, *refs, n_steps, npp, nh, n_new, lam_init):
    kc_refs, vc_refs = refs[:npp], refs[npp:2 * npp]
    kn_ref, vn_ref, lq_ref, g_ref, o_ref, m_ref, l_ref, acc_ref = refs[2 * npp:]
    p = pl.program_id(1)
    rows = q_ref.shape[0]
    q = q_ref[...].astype(BF16)

    @pl.when(p == 0)
    def _():
        m_ref[...] = jnp.full(m_ref.shape, -jnp.inf, F32)
        l_ref[...] = jnp.zeros(l_ref.shape, F32)
        acc_ref[...] = jnp.zeros(acc_ref.shape, F32)

    def update(kv_pairs, biases):
        s_list = [lax.dot_general(q, kf.astype(BF16), (((1,), (1,)), ((), ())), preferred_element_type=F32) + b
                  for (kf, _), b in zip(kv_pairs, biases)]
        m_old = m_ref[...]
        s_max = functools.reduce(jnp.maximum, [jnp.max(s, axis=-1, keepdims=True) for s in s_list])
        m_new = jnp.maximum(m_old, s_max)
        alpha = jnp.exp(m_old - m_new)
        l_new = alpha * l_ref[...]
        acc = alpha * acc_ref[...]
        for s, (_, vf) in zip(s_list, kv_pairs):
            pr = jnp.exp(s - jnp.concatenate([m_new] * (s.shape[1] // LANES), axis=1))
            l_new = l_new + jnp.sum(pr, axis=-1, keepdims=True)
            acc = acc + jnp.dot(pr.astype(BF16), vf.astype(BF16), preferred_element_type=F32)
        m_ref[...] = m_new
        l_ref[...] = l_new
        acc_ref[...] = acc

    @pl.when(p < n_steps)
    def _():
        cols = kc_refs[0].shape[0] * nh
        bias = bias_ref[...]
        update([(kr[...].reshape(cols, HEAD_W), vr[...].reshape(cols, HEAD_W))
                for kr, vr in zip(kc_refs, vc_refs)], [bias] * npp)

    @pl.when(p == n_steps)
    def _():
        cols = kn_ref.shape[0]
        r = lax.broadcasted_iota(jnp.int32, (rows, cols), 0)
        c = lax.broadcasted_iota(jnp.int32, (rows, cols), 1)
        keep = jnp.logical_and((c % nh) == (r // 8), (c // nh) <= (r % n_new))
        update([(kn_ref[...], vn_ref[...])], [jnp.where(keep, 0.0, -jnp.inf)])
        lam = _lam_value(lq_ref, lam_init)
        d = acc_ref[...] / l_ref[...]
        o = d - lam * pltpu.roll(d, rows - n_new, axis=0)
        o_ref[...] = _head_norm(o, g_ref[...], lam_init).astype(o_ref.dtype)


def attn_sample(qrows, cache_k, cache_v, knew, vnew, page_table, lambda_qk, subln_g, lam_init, layer, n_new):
    nb, rows, _ = qrows.shape
    nh = rows // 8
    assert 2 * n_new == 8
    n_pages = page_table.shape[1]
    page = cache_k.shape[2]
    npp = PAGES_PER_STEP
    assert n_pages % npp == 0
    n_steps = n_pages // npp
    cols = page * nh
    bias = jnp.where((jnp.arange(cols)[None, :] % nh) == (jnp.arange(rows)[:, None] // 8), 0.0, -jnp.inf).astype(F32)

    def cache_spec(i):
        def cache_map(b, p, pt):
            return (layer, pt[b, jnp.minimum(p, n_steps - 1) * npp + i], 0, 0, 0)
        return pl.BlockSpec((None, None, page, nh, HEAD_W), cache_map)

    per_b = lambda shape: pl.BlockSpec((None,) + shape, lambda b, p, pt: (b, 0, 0))
    const = lambda shape: pl.BlockSpec(shape, lambda b, p, pt: (0, 0))
    grid_spec = pltpu.PrefetchScalarGridSpec(
        num_scalar_prefetch=1,
        grid=(nb, n_steps + 1),
        in_specs=[per_b((rows, HEAD_W)), const((rows, cols))]
                 + [cache_spec(i) for i in range(npp)] * 2
                 + [per_b((8 * nh, HEAD_W)), per_b((8 * nh, HEAD_W)), const((4, D_HEAD)), const((1, HEAD_W))],
        out_specs=per_b((rows, HEAD_W)),
        scratch_shapes=[pltpu.VMEM((rows, LANES), F32), pltpu.VMEM((rows, LANES), F32),
                        pltpu.VMEM((rows, HEAD_W), F32)],
    )
    return pl.pallas_call(
        functools.partial(_attn_sample_body, n_steps=n_steps, npp=npp, nh=nh, n_new=n_new, lam_init=lam_init),
        grid_spec=grid_spec,
        out_shape=jax.ShapeDtypeStruct((nb, rows, HEAD_W), F32),
        compiler_params=_cparams(("parallel", "arbitrary")),
        name="attn_sample",
    )(page_table, qrows, bias, *([cache_k] * npp), *([cache_v] * npp), knew, vnew, lambda_qk,
      subln_g.reshape(1, HEAD_W))


def _split3(x):
    hi = x.astype(BF16)
    r1 = x - hi.astype(F32)
    mid = r1.astype(BF16)
    lo = (r1 - mid.astype(F32)).astype(BF16)
    return hi, mid, lo


def _silu(x):
    return x * (1.0 / (1.0 + jnp.exp(-x)))


def _ssd_body(xbc_ref, z_ref, dtr_ref, cprev_ref, h0_ref, cw_ref, cb_ref, dtb_ref, alog_ref,
              dskip_ref, ng_ref, e64_ref, et_ref, y_ref, hout_ref,
              tail_ref, st_ref, xpad_ref, *, valid, nheads):
    c = pl.program_id(1)
    nc = pl.num_programs(1)
    T = xbc_ref.shape[0]
    width = z_ref.shape[1]
    gn = SSM_GROUPS * SSM_STATE
    hpg = nheads // SSM_GROUPS
    gw = hpg * SSM_HEAD_DIM
    ntile = width // LANES

    @pl.when(c == 0)
    def _():
        tail_ref[...] = cprev_ref[...]
        h0 = h0_ref[...].reshape(width, SSM_STATE)
        for j in range(ntile):
            st_ref[:, j * LANES:(j + 1) * LANES] = h0[j * LANES:(j + 1) * LANES, :].T

    xbc = xbc_ref[...]
    xpad_ref[0:SUBLANES, :] = tail_ref[...]
    xpad_ref[SUBLANES:SUBLANES + T, :] = xbc
    tail_ref[...] = xbc[T - SUBLANES:, :]
    acc = cb_ref[...] + xbc * cw_ref[CONV_K - 1:CONV_K, :]
    for j in range(CONV_K - 1):
        off = SUBLANES - (CONV_K - 1) + j
        acc = acc + xpad_ref[off:off + T, :] * cw_ref[j:j + 1, :]
    xc = _silu(acc)
    xs = xc[:, :width]
    bm = xc[:, width:width + gn].astype(BF16)
    cm = xc[:, width + gn:].astype(BF16)

    lane = lax.broadcasted_iota(jnp.int32, (T, LANES), 1)
    row = lax.broadcasted_iota(jnp.int32, (T, LANES), 0)
    live = jnp.logical_and(lane < nheads, row < valid)
    xdt = dtr_ref[...] + dtb_ref[...]
    dt = jnp.where(live, jnp.maximum(xdt, 0.0) + jnp.log1p(jnp.exp(-jnp.abs(xdt))), 0.0)
    da = dt * (-jnp.exp(alog_ref[...]))
    tr = lax.broadcasted_iota(jnp.int32, (T, T), 0)
    tc = lax.broadcasted_iota(jnp.int32, (T, T), 1)
    causal = tr >= tc
    ltri = jnp.where(causal, 1.0, 0.0).astype(BF16)
    cs = jnp.dot(jnp.concatenate([ltri, ltri, ltri], axis=1),
                 jnp.concatenate(_split3(da), axis=0), preferred_element_type=F32)
    cs_t = cs.T

    stack = jnp.concatenate([dt, cs], axis=0)
    exp64 = jnp.dot(jnp.concatenate(_split3(stack), axis=1), e64_ref[...], preferred_element_type=F32)
    dt_x = exp64[:T]
    cs_x = exp64[T:]
    cs_b = jnp.dot(jnp.concatenate(_split3(cs), axis=1), et_ref[...], preferred_element_type=F32)

    cs_last = cs_x[T - 1:T, :]
    ecs_x = jnp.exp(cs_x)
    dte_x = jnp.exp(cs_last - cs_x)
    x = xs * dt_x
    xb = x.astype(BF16)
    xdb = (x * dte_x).astype(BF16)

    lane2 = lax.broadcasted_iota(jnp.int32, (T, LANES), 1)
    y_parts = []
    for g in range(SSM_GROUPS):
        bg = bm[:, g * SSM_STATE:(g + 1) * SSM_STATE]
        cg = cm[:, g * SSM_STATE:(g + 1) * SSM_STATE]
        cb = lax.dot_general(cg, bg, (((1,), (1,)), ((), ())), preferred_element_type=F32)
        st_g = st_ref[:, g * gw:(g + 1) * gw]
        y_off = jnp.dot(cg, st_g.astype(BF16), preferred_element_type=F32) * ecs_x[:, g * gw:(g + 1) * gw]
        y_diag = []
        for pr in range(gw // LANES):
            mats = []
            for hh in range(LANES // SSM_HEAD_DIM):
                h = g * hpg + pr * (LANES // SSM_HEAD_DIM) + hh
                seg = cs_b[:, h * T:(h + 1) * T] - cs_t[h:h + 1, :]
                lmat = jnp.exp(jnp.where(causal, seg, -jnp.inf))
                mats.append((cb * lmat).astype(BF16))
            tile = g * (gw // LANES) + pr
            x2 = xb[:, tile * LANES:(tile + 1) * LANES]
            zero = jnp.zeros_like(x2)
            xbd = jnp.concatenate([jnp.where(lane2 < SSM_HEAD_DIM, x2, zero),
                                   jnp.where(lane2 < SSM_HEAD_DIM, zero, x2)], axis=0)
            y_diag.append(jnp.dot(jnp.concatenate(mats, axis=1), xbd, preferred_element_type=F32))
        y_parts.append(jnp.concatenate(y_diag, axis=1) + y_off)
        bg_t = xc[:, width + g * SSM_STATE:width + (g + 1) * SSM_STATE].T.astype(BF16)
        new_states = jnp.dot(bg_t, xdb[:, g * gw:(g + 1) * gw], preferred_element_type=F32)
        decay = ecs_x[T - 1:T, g * gw:(g + 1) * gw]
        st_ref[:, g * gw:(g + 1) * gw] = st_g * decay + new_states

    y = jnp.concatenate(y_parts, axis=1) + dskip_ref[...] * xs
    y = y * _silu(z_ref[...])
    outs = []
    for g in range(SSM_GROUPS):
        yg = y[:, g * gw:(g + 1) * gw]
        ms = jnp.mean(yg * yg, axis=-1, keepdims=True)
        outs.append(yg * lax.rsqrt(ms + NORM_EPS))
    y_ref[...] = (jnp.concatenate(outs, axis=1) * ng_ref[...]).astype(y_ref.dtype)

    @pl.when(c == nc - 1)
    def _():
        for j in range(ntile):
            hout_ref[j * LANES // SSM_HEAD_DIM:(j + 1) * LANES // SSM_HEAD_DIM] = (
                st_ref[:, j * LANES:(j + 1) * LANES].T.reshape(LANES // SSM_HEAD_DIM, SSM_HEAD_DIM, SSM_STATE))


def ssd_mixer(xbc, z, dtr, conv_prev, h0, conv_w, conv_b, dt_bias, a_log, d_skip, norm_g,
              batch, valid):
    mtot, conv_dim = xbc.shape
    width = z.shape[1]
    nheads = width // SSM_HEAD_DIM
    T = SSD_CHUNK
    nc = mtot // batch // T
    hid = jnp.arange(LANES)[:, None]
    e64 = (jnp.arange(width)[None, :] // SSM_HEAD_DIM == hid).astype(BF16)
    et = (jnp.arange(nheads * T)[None, :] // T == hid).astype(BF16)
    e64 = jnp.concatenate([e64] * 3, axis=0)
    et = jnp.concatenate([et] * 3, axis=0)
    pad_l = lambda v: jnp.pad(v.astype(F32), (0, LANES - nheads)).reshape(1, LANES)
    cw = jnp.pad(conv_w, ((0, SUBLANES - CONV_K), (0, 0)))
    full = lambda shape: pl.BlockSpec(shape, lambda b, c: (0,) * len(shape))
    y, hout = pl.pallas_call(
        functools.partial(_ssd_body, valid=valid, nheads=nheads),
        grid=(batch, nc),
        in_specs=[pl.BlockSpec((T, conv_dim), lambda b, c: (b * nc + c, 0)),
                  pl.BlockSpec((T, width), lambda b, c: (b * nc + c, 0)),
                  pl.BlockSpec((T, LANES), lambda b, c: (b * nc + c, 0)),
                  pl.BlockSpec((None, SUBLANES, conv_dim), lambda b, c: (b, 0, 0)),
                  pl.BlockSpec((None, nheads, SSM_HEAD_DIM, SSM_STATE), lambda b, c: (b, 0, 0, 0)),
                  full((SUBLANES, conv_dim)), full((1, conv_dim)), full((1, LANES)), full((1, LANES)),
                  full((1, width)), full((1, width)), full((3 * LANES, width)), full((3 * LANES, nheads * T))],
        out_specs=[pl.BlockSpec((T, width), lambda b, c: (b * nc + c, 0)),
                   pl.BlockSpec((None, nheads, SSM_HEAD_DIM, SSM_STATE), lambda b, c: (b, 0, 0, 0))],
        out_shape=[jax.ShapeDtypeStruct((mtot, width), BF16),
                   jax.ShapeDtypeStruct((batch, nheads, SSM_HEAD_DIM, SSM_STATE), F32)],
        scratch_shapes=[pltpu.VMEM((SUBLANES, conv_dim), F32),
                        pltpu.VMEM((SSM_STATE, width), F32),
                        pltpu.VMEM((SUBLANES + T, conv_dim), F32)],
        compiler_params=_cparams(("parallel", "arbitrary")),
        name="ssd_mixer",
    )(xbc, z, dtr, conv_prev, h0, cw, conv_b.reshape(1, conv_dim), pad_l(dt_bias), pad_l(a_log),
      jnp.repeat(d_skip.astype(F32), SSM_HEAD_DIM).reshape(1, width), norm_g.reshape(1, width), e64, et)
    return y, hout


def _layer(x_p, x_s, l, lam_init, dims, cache_k, cache_v, state_conv, state_ssm, page_table, w, kv_bufs):
    bp, seq, db, dseq = dims
    att_w, ssm_w, conv_dim, nheads = w["att_w"], w["ssm_w"], w["conv_dim"], w["nheads"]
    nh = att_w // HEAD_W
    offs = {"q": 0, "k": att_w, "v": 2 * att_w, "z": 3 * att_w, "xbc": 3 * att_w + ssm_w}
    ssm_args = (w["conv_w"][l], w["conv_b"][l], w["dt_bias"][l], w["a_log"][l], w["d_skip"][l],
                w["ssm_norm_g"][l])

    u_p = rmsnorm(x_p, w["attn_norm_g"][l], BF16)
    u_s = rmsnorm(x_s, w["attn_norm_g"][l], BF16)
    mm = functools.partial(matmul, u_p, u_s, w["w_in_t"], l, w_t=True)
    q_p, q_s = mm(n_off=offs["q"], n=att_w, scale=D_HEAD ** -0.5)
    k_all, k_s = mm(n_off=offs["k"], n=att_w, stacked=True, out_buf=kv_bufs[0])
    v_all, v_s = mm(n_off=offs["v"], n=att_w, stacked=True, out_buf=kv_bufs[1])
    z_p, z_s = mm(n_off=offs["z"], n=ssm_w)
    xbc_p, xbc_s = mm(n_off=offs["xbc"], n=conv_dim)
    dtr_p, dtr_s = matmul(u_p, u_s, w["w_dt_t"], l, w_t=True)

    o_p = attn_prompt(q_p, k_all, v_all, l, w["lambda_qk"][l], w["subln_g"][l], lam_init, bp, seq)
    y_p, ssm_p = ssd_mixer(xbc_p, z_p, dtr_p, jnp.zeros((bp, SUBLANES, conv_dim), F32),
                           jnp.zeros((bp, nheads, SSM_HEAD_DIM, SSM_STATE), F32), *ssm_args,
                           batch=bp, valid=SSD_CHUNK)
    conv_p = xbc_p.reshape(bp, seq, conv_dim)[:, seq - (CONV_K - 1):]

    q5 = q_s.reshape(db, dseq, nh, 2, D_HEAD).transpose(0, 2, 1, 3, 4)
    sel = jnp.eye(2, dtype=F32)[None, None, :, None, :, None]
    qrows = (q5[:, :, None] * sel).reshape(db, nh * 8, HEAD_W)

    def tok_rows(t):
        t = t.reshape(db, dseq, nh, HEAD_W)
        return jnp.pad(t, ((0, 0), (0, SUBLANES - dseq), (0, 0), (0, 0))).reshape(db, SUBLANES * nh, HEAD_W)

    o_s = attn_sample(qrows, cache_k, cache_v, tok_rows(k_s), tok_rows(v_s), page_table,
                      w["lambda_qk"][l], w["subln_g"][l], lam_init, l, dseq)
    o_s = o_s.reshape(db, nh, 8, HEAD_W)[:, :, :dseq].transpose(0, 2, 1, 3).reshape(db * dseq, att_w)

    def pad_chunk(t):
        t = t.reshape(db, dseq, t.shape[-1])
        return jnp.pad(t, ((0, 0), (0, SSD_CHUNK - dseq), (0, 0))).reshape(db * SSD_CHUNK, t.shape[-1])

    cprev = jnp.pad(state_conv[l], ((0, 0), (SUBLANES - (CONV_K - 1), 0), (0, 0)))
    y_s, ssm_s = ssd_mixer(pad_chunk(xbc_s), pad_chunk(z_s), pad_chunk(dtr_s), cprev, state_ssm[l], *ssm_args,
                           batch=db, valid=dseq)
    y_s = y_s.reshape(db, SSD_CHUNK, ssm_w)[:, :dseq].reshape(db * dseq, ssm_w)
    conv_s = jnp.concatenate([state_conv[l], xbc_s.reshape(db, dseq, conv_dim)], axis=1)[:, -(CONV_K - 1):]

    h_p, h_s = matmul((o_p, y_p), (o_s.astype(BF16), y_s), w["w_out"], l, res=(x_p, x_s))
    m_p = rmsnorm(h_p, w["mlp_norm_g"][l], BF16)
    m_s = rmsnorm(h_s, w["mlp_norm_g"][l], BF16)
    a_p, a_s = matmul(m_p, m_s, w["w_up"], l, out_dtype=BF16, act="relu2")
    x_p, x_s = matmul(a_p, a_s, w["w_down"], l, res=(h_p, h_s), tk=2048)

    outs = (conv_p, ssm_p, k_s.reshape(db, dseq, nh, HEAD_W), v_s.reshape(db, dseq, nh, HEAD_W), conv_s, ssm_s)
    return x_p, x_s, (k_all, v_all), outs


def kernel(x_prompt, x_sample, cache_k, cache_v, state_conv, state_ssm, page_table, attn_norm_g, w_in,
           lambda_qk, subln_g, conv_w, conv_b, dt_bias, a_log, d_skip, ssm_norm_g, w_out, mlp_norm_g,
           w_up, w_down, final_norm_g):
    bp, seq, d_model = x_prompt.shape
    db, dseq, _ = x_sample.shape
    depth = w_in.shape[0]
    nheads = dt_bias.shape[1]
    ssm_w = ssm_norm_g.shape[1]
    att_w = d_model - ssm_w
    nh = att_w // HEAD_W
    conv_dim = conv_w.shape[2]
    main_cols = 3 * att_w + ssm_w + conv_dim
    w_in_t = jnp.swapaxes(w_in, 1, 2)
    w = dict(att_w=att_w, ssm_w=ssm_w, conv_dim=conv_dim, nheads=nheads,
             w_in_t=w_in_t, w_dt_t=jnp.pad(w_in_t[:, main_cols:, :], ((0, 0), (0, LANES - nheads), (0, 0))),
             w_out=w_out, w_up=w_up, w_down=w_down,
             attn_norm_g=attn_norm_g, mlp_norm_g=mlp_norm_g, lambda_qk=lambda_qk, subln_g=subln_g,
             conv_w=conv_w, conv_b=conv_b, dt_bias=dt_bias, a_log=a_log, d_skip=d_skip, ssm_norm_g=ssm_norm_g)
    x_p = x_prompt.reshape(bp * seq, d_model)
    x_s = x_sample.reshape(db * dseq, d_model)
    kv_bufs = (jnp.zeros((depth, bp * seq, att_w), F32), jnp.zeros((depth, bp * seq, att_w), F32))
    per_layer = []
    for l in range(depth):
        lam_init = 0.8 - 0.6 * math.exp(-0.3 * l)
        x_p, x_s, kv_bufs, outs = _layer(x_p, x_s, l, lam_init, (bp, seq, db, dseq), cache_k, cache_v,
                                         state_conv, state_ssm, page_table, w, kv_bufs)
        per_layer.append(outs)
    y_p = rmsnorm(x_p, final_norm_g, F32).reshape(bp, seq, d_model)
    y_s = rmsnorm(x_s, final_norm_g, F32).reshape(db, dseq, d_model)
    conv_p, ssm_p, k_s, v_s, conv_s, ssm_s = [jnp.stack([per_layer[l][i] for l in range(depth)])
                                              for i in range(6)]
    k_p = kv_bufs[0].reshape(depth, bp, seq, nh, HEAD_W)
    v_p = kv_bufs[1].reshape(depth, bp, seq, nh, HEAD_W)
    return (y_p, y_s, k_p, v_p, conv_p, ssm_p, k_s, v_s, conv_s, ssm_s)
```

```python
import functools
import math

import jax
import jax.numpy as jnp
from jax import lax
from jax.experimental import pallas as pl
from jax.experimental.pallas import tpu as pltpu

D_HEAD = 64
HEAD_W = 2 * D_HEAD
SSM_HEAD_DIM = 64
SSM_GROUPS = 8
SSM_STATE = 128
CONV_K = 4
SSD_CHUNK = 128
NORM_EPS = 1e-6
LANES = 128
SUBLANES = 8
VMEM_LIMIT = 56 * 1024 * 1024
PAGES_PER_STEP = 8

F32 = jnp.float32
BF16 = jnp.bfloat16


def _cparams(sem):
    return pltpu.CompilerParams(dimension_semantics=sem, vmem_limit_bytes=VMEM_LIMIT)


def _rmsnorm_body(x_ref, g_ref, o_ref):
    x = x_ref[...]
    ms = jnp.mean(x * x, axis=-1, keepdims=True)
    o_ref[...] = (x * lax.rsqrt(ms + NORM_EPS) * g_ref[...]).astype(o_ref.dtype)


def rmsnorm(x, g, out_dtype, rows=256):
    m, d = x.shape
    rows = min(rows, m)
    return pl.pallas_call(
        _rmsnorm_body,
        grid=(m // rows,),
        in_specs=[pl.BlockSpec((rows, d), lambda i: (i, 0)),
                  pl.BlockSpec((1, d), lambda i: (0, 0))],
        out_specs=pl.BlockSpec((rows, d), lambda i: (i, 0)),
        out_shape=jax.ShapeDtypeStruct((m, d), out_dtype),
        compiler_params=_cparams(("parallel",)),
        name="rmsnorm",
    )(x, g.reshape(1, d))


def _mm_body(*refs, n_a, nk, act, scale, has_res, has_buf, w_t):
    ap_refs, as_refs, w_ref = refs[:n_a], refs[n_a:2 * n_a], refs[2 * n_a]
    pos = 2 * n_a + 1
    rp_ref = rs_ref = None
    if has_res:
        rp_ref, rs_ref = refs[pos:pos + 2]
        pos += 2
    pos += int(has_buf)
    op_ref, os_ref, wb_ref = refs[pos:pos + 3]
    scr = refs[pos + 3:]
    k, i = pl.program_id(1), pl.program_id(2)

    @pl.when(i == 0)
    def _():
        if w_t:
            for c in range(w_ref.shape[0] // LANES):
                wb_ref[:, c * LANES:(c + 1) * LANES] = w_ref[c * LANES:(c + 1) * LANES, :].T.astype(BF16)
        else:
            wb_ref[...] = w_ref[...].astype(BF16)

    def lhs_dot(a_refs):
        off, tot = 0, None
        for a_ref in a_refs:
            kk = a_ref.shape[1]
            d = jnp.dot(a_ref[...], wb_ref[off:off + kk, :], preferred_element_type=F32)
            tot = d if tot is None else tot + d
            off += kk
        return tot

    def finish(acc, r_ref, o_ref):
        if act == "relu2":
            acc = jnp.square(jnp.maximum(acc, 0.0))
        if scale != 1.0:
            acc = acc * scale
        if r_ref is not None:
            acc = acc + r_ref[...]
        o_ref[...] = acc.astype(o_ref.dtype)

    def accumulate(a_refs, acc_ref, r_ref, o_ref):
        if nk == 1:
            finish(lhs_dot(a_refs), r_ref, o_ref)
            return

        @pl.when(k == 0)
        def _():
            acc_ref[...] = lhs_dot(a_refs)

        @pl.when(jnp.logical_and(k > 0, k < nk - 1))
        def _():
            acc_ref[...] = acc_ref[...] + lhs_dot(a_refs)

        @pl.when(k == nk - 1)
        def _():
            finish(acc_ref[...] + lhs_dot(a_refs), r_ref, o_ref)

    accumulate(ap_refs, scr[0].at[i] if nk > 1 else None, rp_ref, op_ref)

    @pl.when(i == 0)
    def _():
        accumulate(as_refs, scr[1] if nk > 1 else None, rs_ref, os_ref)


def matmul(a_p, a_s, w, layer, *, n_off=0, n=None, out_dtype=F32, act=None, scale=1.0, res=None, out_buf=None,
           stacked=False, w_t=False, tm=1024, tn=512, tk=4096):
    ap_list = a_p if isinstance(a_p, tuple) else (a_p,)
    as_list = a_s if isinstance(a_s, tuple) else (a_s,)
    n_a = len(ap_list)
    mp, ms = ap_list[0].shape[0], as_list[0].shape[0]
    kdim = sum(x.shape[1] for x in ap_list)
    n = w.shape[1 if w_t else 2] - n_off if n is None else n
    tm, tn = min(tm, mp), min(tn, n)
    tk = kdim if n_a > 1 else min(tk, kdim)
    assert mp % tm == 0 and n % tn == 0 and kdim % tk == 0 and n_off % tn == 0
    ni, nj, nk = mp // tm, n // tn, kdim // tk
    joff = n_off // tn
    row = (lambda i, k: i) if nk == 1 else (lambda i, k: jnp.where(k == nk - 1, i, 0))
    if n_a > 1:
        in_specs = [pl.BlockSpec((tm, x.shape[1]), lambda j, k, i: (i, 0)) for x in ap_list]
        in_specs += [pl.BlockSpec((ms, x.shape[1]), lambda j, k, i: (0, 0)) for x in as_list]
    else:
        in_specs = [pl.BlockSpec((tm, tk), lambda j, k, i: (i, k)),
                    pl.BlockSpec((ms, tk), lambda j, k, i: (0, k))]
    if w_t:
        in_specs.append(pl.BlockSpec((None, tn, tk), lambda j, k, i: (layer, j + joff, k)))
    else:
        in_specs.append(pl.BlockSpec((None, tk, tn), lambda j, k, i: (layer, k, j + joff)))
    args = list(ap_list) + list(as_list) + [w]
    if res is not None:
        in_specs += [pl.BlockSpec((tm, tn), lambda j, k, i: (row(i, k), j)),
                     pl.BlockSpec((ms, tn), lambda j, k, i: (0, j))]
        args += list(res)
    aliases = {}
    if stacked:
        if out_buf is not None:
            in_specs.append(pl.BlockSpec(memory_space=pl.ANY))
            aliases = {len(args): 0}
            args.append(out_buf)
        out_p_spec = pl.BlockSpec((None, tm, tn), lambda j, k, i: (layer, row(i, k), j))
        out_p_shape = jax.ShapeDtypeStruct((w.shape[0], mp, n), out_dtype)
    else:
        out_p_spec = pl.BlockSpec((tm, tn), lambda j, k, i: (row(i, k), j))
        out_p_shape = jax.ShapeDtypeStruct((mp, n), out_dtype)
    scratch = [pltpu.VMEM((tk, tn), BF16)]
    if nk > 1:
        scratch += [pltpu.VMEM((ni, tm, tn), F32), pltpu.VMEM((ms, tn), F32)]
    return pl.pallas_call(
        functools.partial(_mm_body, n_a=n_a, nk=nk, act=act, scale=scale,
                          has_res=res is not None, has_buf=out_buf is not None, w_t=w_t),
        grid=(nj, nk, ni),
        in_specs=in_specs,
        out_specs=[out_p_spec, pl.BlockSpec((ms, tn), lambda j, k, i: (0, j))],
        out_shape=[out_p_shape, jax.ShapeDtypeStruct((ms, n), out_dtype)],
        scratch_shapes=scratch,
        input_output_aliases=aliases,
        compiler_params=_cparams(("parallel", "arbitrary", "arbitrary")),
        name="matmul",
    )(*args)


def _lam_value(lq_ref, lam_init):
    lq = lq_ref[...]
    a = jnp.sum(lq[0:1] * lq[1:2], axis=-1, keepdims=True)
    b = jnp.sum(lq[2:3] * lq[3:4], axis=-1, keepdims=True)
    return jnp.exp(a) - jnp.exp(b) + lam_init


def _head_norm(o, g, lam_init):
    ms = jnp.mean(o * o, axis=-1, keepdims=True)
    return o * lax.rsqrt(ms + NORM_EPS) * g * (1.0 - lam_init)


def _attn_prompt_body(q_ref, k_ref, v_ref, lq_ref, g_ref, o_ref, kb_ref, vt_ref, st_ref, acc_ref, l_ref,
                      *, bq, lam_init):
    seq = q_ref.shape[0]
    nq = seq // bq
    kb_ref[...] = k_ref[...].astype(BF16)
    for j in range(nq):
        vt_ref[j] = v_ref[j * bq:(j + 1) * bq, :].T.astype(BF16)
    lam = _lam_value(lq_ref, lam_init)
    sub = lax.broadcasted_iota(jnp.int32, (HEAD_W, bq), 0)
    key_row = lax.broadcasted_iota(jnp.int32, (bq, 2 * bq), 0)
    q_col = lax.broadcasted_iota(jnp.int32, (bq, 2 * bq), 1)
    keep = key_row <= jnp.where(q_col >= bq, q_col - bq, q_col)

    def fold(x):
        return x.reshape(bq // SUBLANES, SUBLANES, 2 * bq)

    def qblock(i, carry):
        qs = pl.multiple_of(i * bq, bq)
        qt = q_ref[pl.ds(qs, bq), :].T
        zero = jnp.zeros_like(qt)
        q2t = jnp.concatenate([jnp.where(sub < D_HEAD, qt, zero), jnp.where(sub < D_HEAD, zero, qt)],
                              axis=1).astype(BF16)

        def scores(j):
            ks = pl.multiple_of(j * bq, bq)
            return ks, jnp.dot(kb_ref[pl.ds(ks, bq), :], q2t, preferred_element_type=F32)

        def pass_a(j, m_run):
            ks, s = scores(j)
            st_ref[pl.ds(ks, bq), :] = s
            return jnp.maximum(m_run, jnp.max(fold(s), axis=0))

        m_run = lax.fori_loop(0, i, pass_a, jnp.full((SUBLANES, 2 * bq), -jnp.inf, F32))
        _, s = scores(i)
        s = jnp.where(keep, s, -jnp.inf)
        st_ref[pl.ds(qs, bq), :] = s
        m_run = jnp.maximum(m_run, jnp.max(fold(s), axis=0))
        m = jnp.max(m_run, axis=0, keepdims=True)

        acc_ref[...] = jnp.zeros(acc_ref.shape, F32)
        l_ref[...] = jnp.zeros(l_ref.shape, F32)

        def pass_b(j, c):
            ks = pl.multiple_of(j * bq, bq)
            p = jnp.exp(st_ref[pl.ds(ks, bq), :] - m)
            l_ref[...] += jnp.sum(fold(p), axis=0)
            acc_ref[...] += jnp.dot(vt_ref[j], p.astype(BF16), preferred_element_type=F32)
            return c

        lax.fori_loop(0, i + 1, pass_b, 0)
        l = jnp.sum(l_ref[...], axis=0, keepdims=True)
        d = acc_ref[...] * (1.0 / l)
        o = (d[:, :bq] - lam * d[:, bq:]).T
        o_ref[pl.ds(qs, bq), :] = _head_norm(o, g_ref[...], lam_init).astype(o_ref.dtype)
        return carry

    lax.fori_loop(0, nq, qblock, 0)


def attn_prompt(q, k_all, v_all, layer, lambda_qk, subln_g, lam_init, batch, seq, bq=512):
    m, width = q.shape
    nh = width // HEAD_W
    kv_spec = pl.BlockSpec((None, seq, HEAD_W), lambda b, h: (layer, b, h))
    return pl.pallas_call(
        functools.partial(_attn_prompt_body, bq=bq, lam_init=lam_init),
        grid=(batch, nh),
        in_specs=[pl.BlockSpec((seq, HEAD_W), lambda b, h: (b, h)), kv_spec, kv_spec,
                  pl.BlockSpec((4, D_HEAD), lambda b, h: (0, 0)),
                  pl.BlockSpec((1, HEAD_W), lambda b, h: (0, 0))],
        out_specs=pl.BlockSpec((seq, HEAD_W), lambda b, h: (b, h)),
        out_shape=jax.ShapeDtypeStruct((m, width), BF16),
        scratch_shapes=[pltpu.VMEM((seq, HEAD_W), BF16),
                        pltpu.VMEM((seq // bq, HEAD_W, bq), BF16),
                        pltpu.VMEM((seq, 2 * bq), F32),
                        pltpu.VMEM((HEAD_W, 2 * bq), F32),
                        pltpu.VMEM((SUBLANES, 2 * bq), F32)],
        compiler_params=_cparams(("parallel", "parallel")),
        name="attn_prompt",
    )(q, k_all, v_all, lambda_qk, subln_g.reshape(1, HEAD_W))


def _attn_sample_body(pt_ref, q_ref, bias_ref, *refs, n_steps, npp, nh, n_new, lam_init):
    kc_refs, vc_refs = refs[:npp], refs[npp:2 * npp]
    kn_ref, vn_ref, lq_ref, g_ref, o_ref, m_ref, l_ref, acc_ref = refs[2 * npp:]
    p = pl.program_id(1)
    rows = q_ref.shape[0]
    q = q_ref[...].astype(BF16)

    @pl.when(p == 0)
    def _():
        m_ref[...] = jnp.full(m_ref.shape, -jnp.inf, F32)
        l_ref[...] = jnp.zeros(l_ref.shape, F32)
        acc_ref[...] = jnp.zeros(acc_ref.shape, F32)

    def update(r0, nr, kv_pairs, biases):
        qg = q[r0:r0 + nr]
        s_list = [lax.dot_general(qg, kf.astype(BF16), (((1,), (1,)), ((), ())), preferred_element_type=F32) + b
                  for (kf, _), b in zip(kv_pairs, biases)]
        m_old = m_ref[r0:r0 + nr]
        s_max = functools.reduce(jnp.maximum, [jnp.max(s, axis=-1, keepdims=True) for s in s_list])
        m_new = jnp.maximum(m_old, s_max)
        alpha = jnp.exp(m_old - m_new)
        l_new = alpha * l_ref[r0:r0 + nr]
        acc = alpha * acc_ref[r0:r0 + nr]
        for s, (_, vf) in zip(s_list, kv_pairs):
            pr = jnp.exp(s - jnp.concatenate([m_new] * (s.shape[1] // LANES), axis=1))
            l_new = l_new + jnp.sum(pr, axis=-1, keepdims=True)
            acc = acc + jnp.dot(pr.astype(BF16), vf.astype(BF16), preferred_element_type=F32)
        m_ref[r0:r0 + nr] = m_new
        l_ref[r0:r0 + nr] = l_new
        acc_ref[r0:r0 + nr] = acc

    @pl.when(p < n_steps)
    def _():
        ntok = kc_refs[0].shape[0]
        bias = bias_ref[...]
        for hg in range(nh // SUBLANES):
            hs = slice(hg * SUBLANES, (hg + 1) * SUBLANES)
            update(hg * 8 * SUBLANES, 8 * SUBLANES,
                   [(kr[:, hs, :].reshape(ntok * SUBLANES, HEAD_W), vr[:, hs, :].reshape(ntok * SUBLANES, HEAD_W))
                    for kr, vr in zip(kc_refs, vc_refs)], [bias] * npp)

    @pl.when(p == n_steps)
    def _():
        cols = kn_ref.shape[0]
        r = lax.broadcasted_iota(jnp.int32, (rows, cols), 0)
        c = lax.broadcasted_iota(jnp.int32, (rows, cols), 1)
        keep = jnp.logical_and((c % nh) == (r // 8), (c // nh) <= (r % n_new))
        update(0, rows, [(kn_ref[...], vn_ref[...])], [jnp.where(keep, 0.0, -jnp.inf)])
        lam = _lam_value(lq_ref, lam_init)
        d = acc_ref[...] / l_ref[...]
        o = d - lam * pltpu.roll(d, rows - n_new, axis=0)
        o_ref[...] = _head_norm(o, g_ref[...], lam_init).astype(o_ref.dtype)


def attn_sample(qrows, cache_k, cache_v, knew, vnew, page_table, lambda_qk, subln_g, lam_init, layer, n_new):
    nb, rows, _ = qrows.shape
    nh = rows // 8
    assert 2 * n_new == 8
    n_pages = page_table.shape[1]
    page = cache_k.shape[2]
    npp = PAGES_PER_STEP
    assert n_pages % npp == 0
    n_steps = n_pages // npp
    assert nh % SUBLANES == 0
    brow, bcol = 8 * SUBLANES, page * SUBLANES
    bias = jnp.where((jnp.arange(bcol)[None, :] % SUBLANES) == (jnp.arange(brow)[:, None] // 8),
                     0.0, -jnp.inf).astype(F32)

    def cache_spec(i):
        def cache_map(b, p, pt):
            return (layer, pt[b, jnp.minimum(p, n_steps - 1) * npp + i], 0, 0, 0)
        return pl.BlockSpec((None, None, page, nh, HEAD_W), cache_map)

    per_b = lambda shape: pl.BlockSpec((None,) + shape, lambda b, p, pt: (b, 0, 0))
    const = lambda shape: pl.BlockSpec(shape, lambda b, p, pt: (0, 0))
    grid_spec = pltpu.PrefetchScalarGridSpec(
        num_scalar_prefetch=1,
        grid=(nb, n_steps + 1),
        in_specs=[per_b((rows, HEAD_W)), const((brow, bcol))]
                 + [cache_spec(i) for i in range(npp)] * 2
                 + [per_b((8 * nh, HEAD_W)), per_b((8 * nh, HEAD_W)), const((4, D_HEAD)), const((1, HEAD_W))],
        out_specs=per_b((rows, HEAD_W)),
        scratch_shapes=[pltpu.VMEM((rows, LANES), F32), pltpu.VMEM((rows, LANES), F32),
                        pltpu.VMEM((rows, HEAD_W), F32)],
    )
    return pl.pallas_call(
        functools.partial(_attn_sample_body, n_steps=n_steps, npp=npp, nh=nh, n_new=n_new, lam_init=lam_init),
        grid_spec=grid_spec,
        out_shape=jax.ShapeDtypeStruct((nb, rows, HEAD_W), F32),
        compiler_params=_cparams(("parallel", "arbitrary")),
        name="attn_sample",
    )(page_table, qrows, bias, *([cache_k] * npp), *([cache_v] * npp), knew, vnew, lambda_qk,
      subln_g.reshape(1, HEAD_W))


def _split3(x):
    hi = x.astype(BF16)
    r1 = x - hi.astype(F32)
    mid = r1.astype(BF16)
    lo = (r1 - mid.astype(F32)).astype(BF16)
    return hi, mid, lo


def _silu(x):
    return x * (1.0 / (1.0 + jnp.exp(-x)))


def _ssd_body(xbc_ref, z_ref, dtr_ref, cprev_ref, h0_ref, cw_ref, cb_ref, dtb_ref, alog_ref,
              dskip_ref, ng_ref, e64_ref, et_ref, y_ref, hout_ref,
              tail_ref, st_ref, xpad_ref, *, valid, nheads):
    c = pl.program_id(1)
    nc = pl.num_programs(1)
    T = xbc_ref.shape[0]
    width = z_ref.shape[1]
    gn = SSM_GROUPS * SSM_STATE
    hpg = nheads // SSM_GROUPS
    gw = hpg * SSM_HEAD_DIM
    ntile = width // LANES

    @pl.when(c == 0)
    def _():
        tail_ref[...] = cprev_ref[...]
        h0 = h0_ref[...].reshape(width, SSM_STATE)
        for j in range(ntile):
            st_ref[:, j * LANES:(j + 1) * LANES] = h0[j * LANES:(j + 1) * LANES, :].T

    xbc = xbc_ref[...]
    xpad_ref[0:SUBLANES, :] = tail_ref[...]
    xpad_ref[SUBLANES:SUBLANES + T, :] = xbc
    tail_ref[...] = xbc[T - SUBLANES:, :]
    acc = cb_ref[...] + xbc * cw_ref[CONV_K - 1:CONV_K, :]
    for j in range(CONV_K - 1):
        off = SUBLANES - (CONV_K - 1) + j
        acc = acc + xpad_ref[off:off + T, :] * cw_ref[j:j + 1, :]
    xc = _silu(acc)
    xs = xc[:, :width]
    bm = xc[:, width:width + gn].astype(BF16)
    cm = xc[:, width + gn:].astype(BF16)

    lane = lax.broadcasted_iota(jnp.int32, (T, LANES), 1)
    row = lax.broadcasted_iota(jnp.int32, (T, LANES), 0)
    live = jnp.logical_and(lane < nheads, row < valid)
    xdt = dtr_ref[...] + dtb_ref[...]
    dt = jnp.where(live, jnp.maximum(xdt, 0.0) + jnp.log1p(jnp.exp(-jnp.abs(xdt))), 0.0)
    da = dt * (-jnp.exp(alog_ref[...]))
    tr = lax.broadcasted_iota(jnp.int32, (T, T), 0)
    tc = lax.broadcasted_iota(jnp.int32, (T, T), 1)
    causal = tr >= tc
    ltri = jnp.where(causal, 1.0, 0.0).astype(BF16)
    cs = jnp.dot(jnp.concatenate([ltri, ltri, ltri], axis=1),
                 jnp.concatenate(_split3(da), axis=0), preferred_element_type=F32)
    cs_t = cs.T

    stack = jnp.concatenate([dt, cs], axis=0)
    exp64 = jnp.dot(jnp.concatenate(_split3(stack), axis=1), e64_ref[...], preferred_element_type=F32)
    dt_x = exp64[:T]
    cs_x = exp64[T:]
    cs_b = jnp.dot(jnp.concatenate(_split3(cs), axis=1), et_ref[...], preferred_element_type=F32)

    cs_last = cs_x[T - 1:T, :]
    ecs_x = jnp.exp(cs_x)
    dte_x = jnp.exp(cs_last - cs_x)
    x = xs * dt_x
    xb = x.astype(BF16)
    xdb = (x * dte_x).astype(BF16)

    lane2 = lax.broadcasted_iota(jnp.int32, (T, LANES), 1)
    y_parts = []
    for g in range(SSM_GROUPS):
        bg = bm[:, g * SSM_STATE:(g + 1) * SSM_STATE]
        cg = cm[:, g * SSM_STATE:(g + 1) * SSM_STATE]
        cb = lax.dot_general(cg, bg, (((1,), (1,)), ((), ())), preferred_element_type=F32)
        st_g = st_ref[:, g * gw:(g + 1) * gw]
        y_off = jnp.dot(cg, st_g.astype(BF16), preferred_element_type=F32) * ecs_x[:, g * gw:(g + 1) * gw]
        y_diag = []
        for pr in range(gw // LANES):
            mats = []
            for hh in range(LANES // SSM_HEAD_DIM):
                h = g * hpg + pr * (LANES // SSM_HEAD_DIM) + hh
                seg = cs_b[:, h * T:(h + 1) * T] - cs_t[h:h + 1, :]
                lmat = jnp.exp(jnp.where(causal, seg, -jnp.inf))
                mats.append((cb * lmat).astype(BF16))
            tile = g * (gw // LANES) + pr
            x2 = xb[:, tile * LANES:(tile + 1) * LANES]
            zero = jnp.zeros_like(x2)
            xbd = jnp.concatenate([jnp.where(lane2 < SSM_HEAD_DIM, x2, zero),
                                   jnp.where(lane2 < SSM_HEAD_DIM, zero, x2)], axis=0)
            y_diag.append(jnp.dot(jnp.concatenate(mats, axis=1), xbd, preferred_element_type=F32))
        y_parts.append(jnp.concatenate(y_diag, axis=1) + y_off)
        bg_t = xc[:, width + g * SSM_STATE:width + (g + 1) * SSM_STATE].T.astype(BF16)
        new_states = jnp.dot(bg_t, xdb[:, g * gw:(g + 1) * gw], preferred_element_type=F32)
        decay = ecs_x[T - 1:T, g * gw:(g + 1) * gw]
        st_ref[:, g * gw:(g + 1) * gw] = st_g * decay + new_states

    y = jnp.concatenate(y_parts, axis=1) + dskip_ref[...] * xs
    y = y * _silu(z_ref[...])
    outs = []
    for g in range(SSM_GROUPS):
        yg = y[:, g * gw:(g + 1) * gw]
        ms = jnp.mean(yg * yg, axis=-1, keepdims=True)
        outs.append(yg * lax.rsqrt(ms + NORM_EPS))
    y_ref[...] = (jnp.concatenate(outs, axis=1) * ng_ref[...]).astype(y_ref.dtype)

    @pl.when(c == nc - 1)
    def _():
        for j in range(ntile):
            hout_ref[j * LANES // SSM_HEAD_DIM:(j + 1) * LANES // SSM_HEAD_DIM] = (
                st_ref[:, j * LANES:(j + 1) * LANES].T.reshape(LANES // SSM_HEAD_DIM, SSM_HEAD_DIM, SSM_STATE))


def ssd_mixer(xbc, z, dtr, conv_prev, h0, conv_w, conv_b, dt_bias, a_log, d_skip, norm_g,
              batch, valid):
    mtot, conv_dim = xbc.shape
    width = z.shape[1]
    nheads = width // SSM_HEAD_DIM
    T = SSD_CHUNK
    nc = mtot // batch // T
    hid = jnp.arange(LANES)[:, None]
    e64 = (jnp.arange(width)[None, :] // SSM_HEAD_DIM == hid).astype(BF16)
    et = (jnp.arange(nheads * T)[None, :] // T == hid).astype(BF16)
    e64 = jnp.concatenate([e64] * 3, axis=0)
    et = jnp.concatenate([et] * 3, axis=0)
    pad_l = lambda v: jnp.pad(v.astype(F32), (0, LANES - nheads)).reshape(1, LANES)
    cw = jnp.pad(conv_w, ((0, SUBLANES - CONV_K), (0, 0)))
    full = lambda shape: pl.BlockSpec(shape, lambda b, c: (0,) * len(shape))
    y, hout = pl.pallas_call(
        functools.partial(_ssd_body, valid=valid, nheads=nheads),
        grid=(batch, nc),
        in_specs=[pl.BlockSpec((T, conv_dim), lambda b, c: (b * nc + c, 0)),
                  pl.BlockSpec((T, width), lambda b, c: (b * nc + c, 0)),
                  pl.BlockSpec((T, LANES), lambda b, c: (b * nc + c, 0)),
                  pl.BlockSpec((None, SUBLANES, conv_dim), lambda b, c: (b, 0, 0)),
                  pl.BlockSpec((None, nheads, SSM_HEAD_DIM, SSM_STATE), lambda b, c: (b, 0, 0, 0)),
                  full((SUBLANES, conv_dim)), full((1, conv_dim)), full((1, LANES)), full((1, LANES)),
                  full((1, width)), full((1, width)), full((3 * LANES, width)), full((3 * LANES, nheads * T))],
        out_specs=[pl.BlockSpec((T, width), lambda b, c: (b * nc + c, 0)),
                   pl.BlockSpec((None, nheads, SSM_HEAD_DIM, SSM_STATE), lambda b, c: (b, 0, 0, 0))],
        out_shape=[jax.ShapeDtypeStruct((mtot, width), BF16),
                   jax.ShapeDtypeStruct((batch, nheads, SSM_HEAD_DIM, SSM_STATE), F32)],
        scratch_shapes=[pltpu.VMEM((SUBLANES, conv_dim), F32),
                        pltpu.VMEM((SSM_STATE, width), F32),
                        pltpu.VMEM((SUBLANES + T, conv_dim), F32)],
        compiler_params=_cparams(("parallel", "arbitrary")),
        name="ssd_mixer",
    )(xbc, z, dtr, conv_prev, h0, cw, conv_b.reshape(1, conv_dim), pad_l(dt_bias), pad_l(a_log),
      jnp.repeat(d_skip.astype(F32), SSM_HEAD_DIM).reshape(1, width), norm_g.reshape(1, width), e64, et)
    return y, hout


def _layer(x_p, x_s, l, lam_init, dims, cache_k, cache_v, state_conv, state_ssm, page_table, w, kv_bufs):
    bp, seq, db, dseq = dims
    att_w, ssm_w, conv_dim, nheads = w["att_w"], w["ssm_w"], w["conv_dim"], w["nheads"]
    nh = att_w // HEAD_W
    offs = {"q": 0, "k": att_w, "v": 2 * att_w, "z": 3 * att_w, "xbc": 3 * att_w + ssm_w}
    ssm_args = (w["conv_w"][l], w["conv_b"][l], w["dt_bias"][l], w["a_log"][l], w["d_skip"][l],
                w["ssm_norm_g"][l])

    u_p = rmsnorm(x_p, w["attn_norm_g"][l], BF16)
    u_s = rmsnorm(x_s, w["attn_norm_g"][l], BF16)
    mm = functools.partial(matmul, u_p, u_s, w["w_in_t"], l, w_t=True)
    q_p, q_s = mm(n_off=offs["q"], n=att_w, scale=D_HEAD ** -0.5)
    k_all, k_s = mm(n_off=offs["k"], n=att_w, stacked=True, out_buf=kv_bufs[0])
    v_all, v_s = mm(n_off=offs["v"], n=att_w, stacked=True, out_buf=kv_bufs[1])
    z_p, z_s = mm(n_off=offs["z"], n=ssm_w)
    xbc_p, xbc_s = mm(n_off=offs["xbc"], n=conv_dim)
    dtr_p, dtr_s = matmul(u_p, u_s, w["w_dt_t"], l, w_t=True)

    o_p = attn_prompt(q_p, k_all, v_all, l, w["lambda_qk"][l], w["subln_g"][l], lam_init, bp, seq)
    y_p, ssm_p = ssd_mixer(xbc_p, z_p, dtr_p, jnp.zeros((bp, SUBLANES, conv_dim), F32),
                           jnp.zeros((bp, nheads, SSM_HEAD_DIM, SSM_STATE), F32), *ssm_args,
                           batch=bp, valid=SSD_CHUNK)
    conv_p = xbc_p.reshape(bp, seq, conv_dim)[:, seq - (CONV_K - 1):]

    q5 = q_s.reshape(db, dseq, nh, 2, D_HEAD).transpose(0, 2, 1, 3, 4)
    sel = jnp.eye(2, dtype=F32)[None, None, :, None, :, None]
    qrows = (q5[:, :, None] * sel).reshape(db, nh * 8, HEAD_W)

    def tok_rows(t):
        t = t.reshape(db, dseq, nh, HEAD_W)
        return jnp.pad(t, ((0, 0), (0, SUBLANES - dseq), (0, 0), (0, 0))).reshape(db, SUBLANES * nh, HEAD_W)

    o_s = attn_sample(qrows, cache_k, cache_v, tok_rows(k_s), tok_rows(v_s), page_table,
                      w["lambda_qk"][l], w["subln_g"][l], lam_init, l, dseq)
    o_s = o_s.reshape(db, nh, 8, HEAD_W)[:, :, :dseq].transpose(0, 2, 1, 3).reshape(db * dseq, att_w)

    def pad_chunk(t):
        t = t.reshape(db, dseq, t.shape[-1])
        return jnp.pad(t, ((0, 0), (0, SSD_CHUNK - dseq), (0, 0))).reshape(db * SSD_CHUNK, t.shape[-1])

    cprev = jnp.pad(state_conv[l], ((0, 0), (SUBLANES - (CONV_K - 1), 0), (0, 0)))
    y_s, ssm_s = ssd_mixer(pad_chunk(xbc_s), pad_chunk(z_s), pad_chunk(dtr_s), cprev, state_ssm[l], *ssm_args,
                           batch=db, valid=dseq)
    y_s = y_s.reshape(db, SSD_CHUNK, ssm_w)[:, :dseq].reshape(db * dseq, ssm_w)
    conv_s = jnp.concatenate([state_conv[l], xbc_s.reshape(db, dseq, conv_dim)], axis=1)[:, -(CONV_K - 1):]

    h_p, h_s = matmul((o_p, y_p), (o_s.astype(BF16), y_s), w["w_out"], l, res=(x_p, x_s))
    m_p = rmsnorm(h_p, w["mlp_norm_g"][l], BF16)
    m_s = rmsnorm(h_s, w["mlp_norm_g"][l], BF16)
    a_p, a_s = matmul(m_p, m_s, w["w_up"], l, out_dtype=BF16, act="relu2")
    x_p, x_s = matmul(a_p, a_s, w["w_down"], l, res=(h_p, h_s), tk=2048)

    outs = (conv_p, ssm_p, k_s.reshape(db, dseq, nh, HEAD_W), v_s.reshape(db, dseq, nh, HEAD_W), conv_s, ssm_s)
    return x_p, x_s, (k_all, v_all), outs


def kernel(x_prompt, x_sample, cache_k, cache_v, state_conv, state_ssm, page_table, attn_norm_g, w_in,
           lambda_qk, subln_g, conv_w, conv_b, dt_bias, a_log, d_skip, ssm_norm_g, w_out, mlp_norm_g,
           w_up, w_down, final_norm_g):
    bp, seq, d_model = x_prompt.shape
    db, dseq, _ = x_sample.shape
    depth = w_in.shape[0]
    nheads = dt_bias.shape[1]
    ssm_w = ssm_norm_g.shape[1]
    att_w = d_model - ssm_w
    nh = att_w // HEAD_W
    conv_dim = conv_w.shape[2]
    main_cols = 3 * att_w + ssm_w + conv_dim
    w_in_t = jnp.swapaxes(w_in, 1, 2)
    w = dict(att_w=att_w, ssm_w=ssm_w, conv_dim=conv_dim, nheads=nheads,
             w_in_t=w_in_t, w_dt_t=jnp.pad(w_in_t[:, main_cols:, :], ((0, 0), (0, LANES - nheads), (0, 0))),
             w_out=w_out, w_up=w_up, w_down=w_down,
             attn_norm_g=attn_norm_g, mlp_norm_g=mlp_norm_g, lambda_qk=lambda_qk, subln_g=subln_g,
             conv_w=conv_w, conv_b=conv_b, dt_bias=dt_bias, a_log=a_log, d_skip=d_skip, ssm_norm_g=ssm_norm_g)
    x_p = x_prompt.reshape(bp * seq, d_model)
    x_s = x_sample.reshape(db * dseq, d_model)
    kv_bufs = (jnp.zeros((depth, bp * seq, att_w), F32), jnp.zeros((depth, bp * seq, att_w), F32))
    per_layer = []
    for l in range(depth):
        lam_init = 0.8 - 0.6 * math.exp(-0.3 * l)
        x_p, x_s, kv_bufs, outs = _layer(x_p, x_s, l, lam_init, (bp, seq, db, dseq), cache_k, cache_v,
                                         state_conv, state_ssm, page_table, w, kv_bufs)
        per_layer.append(outs)
    y_p = rmsnorm(x_p, final_norm_g, F32).reshape(bp, seq, d_model)
    y_s = rmsnorm(x_s, final_norm_g, F32).reshape(db, dseq, d_model)
    conv_p, ssm_p, k_s, v_s, conv_s, ssm_s = [jnp.stack([per_layer[l][i] for l in range(depth)])
                                              for i in range(6)]
    k_p = kv_bufs[0].reshape(depth, bp, seq, nh, HEAD_W)
    v_p = kv_bufs[1].reshape(depth, bp, seq, nh, HEAD_W)
    return (y_p, y_s, k_p, v_p, conv_p, ssm_p, k_s, v_s, conv_s, ssm_s)
```

```python
import functools
import math

import jax
import jax.numpy as jnp
from jax import lax
from jax.experimental import pallas as pl
from jax.experimental.pallas import tpu as pltpu

D_HEAD = 64
HEAD_W = 2 * D_HEAD
SSM_HEAD_DIM = 64
SSM_GROUPS = 8
SSM_STATE = 128
CONV_K = 4
SSD_CHUNK = 128
NORM_EPS = 1e-6
LANES = 128
SUBLANES = 8
VMEM_LIMIT = 56 * 1024 * 1024
PAGES_PER_STEP = 8

F32 = jnp.float32
BF16 = jnp.bfloat16


def _cparams(sem):
    return pltpu.CompilerParams(dimension_semantics=sem, vmem_limit_bytes=VMEM_LIMIT)


def _rmsnorm_body(x_ref, g_ref, o_ref):
    x = x_ref[...]
    ms = jnp.mean(x * x, axis=-1, keepdims=True)
    o_ref[...] = (x * lax.rsqrt(ms + NORM_EPS) * g_ref[...]).astype(o_ref.dtype)


def rmsnorm(x, g, out_dtype, rows=256):
    m, d = x.shape
    rows = min(rows, m)
    return pl.pallas_call(
        _rmsnorm_body,
        grid=(m // rows,),
        in_specs=[pl.BlockSpec((rows, d), lambda i: (i, 0)),
                  pl.BlockSpec((1, d), lambda i: (0, 0))],
        out_specs=pl.BlockSpec((rows, d), lambda i: (i, 0)),
        out_shape=jax.ShapeDtypeStruct((m, d), out_dtype),
        compiler_params=_cparams(("parallel",)),
        name="rmsnorm",
    )(x, g.reshape(1, d))


def _mm_body(*refs, n_a, nk, act, scale, has_res, has_buf, w_t):
    ap_refs, as_refs, w_ref = refs[:n_a], refs[n_a:2 * n_a], refs[2 * n_a]
    pos = 2 * n_a + 1
    rp_ref = rs_ref = None
    if has_res:
        rp_ref, rs_ref = refs[pos:pos + 2]
        pos += 2
    pos += int(has_buf)
    op_ref, os_ref, wb_ref = refs[pos:pos + 3]
    scr = refs[pos + 3:]
    k, i = pl.program_id(1), pl.program_id(2)

    @pl.when(i == 0)
    def _():
        if w_t:
            for c in range(w_ref.shape[0] // LANES):
                wb_ref[:, c * LANES:(c + 1) * LANES] = w_ref[c * LANES:(c + 1) * LANES, :].T.astype(BF16)
        else:
            wb_ref[...] = w_ref[...].astype(BF16)

    def lhs_dot(a_refs):
        off, tot = 0, None
        for a_ref in a_refs:
            kk = a_ref.shape[1]
            d = jnp.dot(a_ref[...], wb_ref[off:off + kk, :], preferred_element_type=F32)
            tot = d if tot is None else tot + d
            off += kk
        return tot

    def finish(acc, r_ref, o_ref):
        if act == "relu2":
            acc = jnp.square(jnp.maximum(acc, 0.0))
        if scale != 1.0:
            acc = acc * scale
        if r_ref is not None:
            acc = acc + r_ref[...]
        o_ref[...] = acc.astype(o_ref.dtype)

    def accumulate(a_refs, acc_ref, r_ref, o_ref):
        if nk == 1:
            finish(lhs_dot(a_refs), r_ref, o_ref)
            return

        @pl.when(k == 0)
        def _():
            acc_ref[...] = lhs_dot(a_refs)

        @pl.when(jnp.logical_and(k > 0, k < nk - 1))
        def _():
            acc_ref[...] = acc_ref[...] + lhs_dot(a_refs)

        @pl.when(k == nk - 1)
        def _():
            finish(acc_ref[...] + lhs_dot(a_refs), r_ref, o_ref)

    accumulate(ap_refs, scr[0].at[i] if nk > 1 else None, rp_ref, op_ref)

    @pl.when(i == 0)
    def _():
        accumulate(as_refs, scr[1] if nk > 1 else None, rs_ref, os_ref)


def matmul(a_p, a_s, w, layer, *, n_off=0, n=None, out_dtype=F32, act=None, scale=1.0, res=None, out_buf=None,
           stacked=False, w_t=False, tm=1024, tn=512, tk=4096):
    ap_list = a_p if isinstance(a_p, tuple) else (a_p,)
    as_list = a_s if isinstance(a_s, tuple) else (a_s,)
    n_a = len(ap_list)
    mp, ms = ap_list[0].shape[0], as_list[0].shape[0]
    kdim = sum(x.shape[1] for x in ap_list)
    n = w.shape[1 if w_t else 2] - n_off if n is None else n
    tm, tn = min(tm, mp), min(tn, n)
    tk = kdim if n_a > 1 else min(tk, kdim)
    assert mp % tm == 0 and n % tn == 0 and kdim % tk == 0 and n_off % tn == 0
    ni, nj, nk = mp // tm, n // tn, kdim // tk
    joff = n_off // tn
    row = (lambda i, k: i) if nk == 1 else (lambda i, k: jnp.where(k == nk - 1, i, 0))
    if n_a > 1:
        in_specs = [pl.BlockSpec((tm, x.shape[1]), lambda j, k, i: (i, 0)) for x in ap_list]
        in_specs += [pl.BlockSpec((ms, x.shape[1]), lambda j, k, i: (0, 0)) for x in as_list]
    else:
        in_specs = [pl.BlockSpec((tm, tk), lambda j, k, i: (i, k)),
                    pl.BlockSpec((ms, tk), lambda j, k, i: (0, k))]
    def w_tile(j, k, i):
        nxt = jnp.minimum(j * nk + k + jnp.minimum(i, 1), nj * nk - 1)
        return nxt // nk, nxt % nk

    if w_t:
        in_specs.append(pl.BlockSpec((None, tn, tk), lambda j, k, i: (layer, w_tile(j, k, i)[0] + joff,
                                                                      w_tile(j, k, i)[1])))
    else:
        in_specs.append(pl.BlockSpec((None, tk, tn), lambda j, k, i: (layer, w_tile(j, k, i)[1],
                                                                      w_tile(j, k, i)[0] + joff)))
    args = list(ap_list) + list(as_list) + [w]
    if res is not None:
        in_specs += [pl.BlockSpec((tm, tn), lambda j, k, i: (row(i, k), j)),
                     pl.BlockSpec((ms, tn), lambda j, k, i: (0, j))]
        args += list(res)
    aliases = {}
    if stacked:
        if out_buf is not None:
            in_specs.append(pl.BlockSpec(memory_space=pl.ANY))
            aliases = {len(args): 0}
            args.append(out_buf)
        out_p_spec = pl.BlockSpec((None, tm, tn), lambda j, k, i: (layer, row(i, k), j))
        out_p_shape = jax.ShapeDtypeStruct((w.shape[0], mp, n), out_dtype)
    else:
        out_p_spec = pl.BlockSpec((tm, tn), lambda j, k, i: (row(i, k), j))
        out_p_shape = jax.ShapeDtypeStruct((mp, n), out_dtype)
    scratch = [pltpu.VMEM((tk, tn), BF16)]
    if nk > 1:
        scratch += [pltpu.VMEM((ni, tm, tn), F32), pltpu.VMEM((ms, tn), F32)]
    return pl.pallas_call(
        functools.partial(_mm_body, n_a=n_a, nk=nk, act=act, scale=scale,
                          has_res=res is not None, has_buf=out_buf is not None, w_t=w_t),
        grid=(nj, nk, ni),
        in_specs=in_specs,
        out_specs=[out_p_spec, pl.BlockSpec((ms, tn), lambda j, k, i: (0, j))],
        out_shape=[out_p_shape, jax.ShapeDtypeStruct((ms, n), out_dtype)],
        scratch_shapes=scratch,
        input_output_aliases=aliases,
        compiler_params=_cparams(("parallel", "arbitrary", "arbitrary")),
        name="matmul",
    )(*args)


def _lam_value(lq_ref, lam_init):
    lq = lq_ref[...]
    a = jnp.sum(lq[0:1] * lq[1:2], axis=-1, keepdims=True)
    b = jnp.sum(lq[2:3] * lq[3:4], axis=-1, keepdims=True)
    return jnp.exp(a) - jnp.exp(b) + lam_init


def _head_norm(o, g, lam_init):
    ms = jnp.mean(o * o, axis=-1, keepdims=True)
    return o * lax.rsqrt(ms + NORM_EPS) * g * (1.0 - lam_init)


def _attn_prompt_body(q_ref, k_ref, v_ref, lq_ref, g_ref, o_ref, kb_ref, vt_ref, st_ref, acc_ref, l_ref,
                      *, bq, lam_init):
    seq = q_ref.shape[0]
    nq = seq // bq
    kb_ref[...] = k_ref[...].astype(BF16)
    for j in range(nq):
        vt_ref[j] = v_ref[j * bq:(j + 1) * bq, :].T.astype(BF16)
    lam = _lam_value(lq_ref, lam_init)
    sub = lax.broadcasted_iota(jnp.int32, (HEAD_W, bq), 0)
    key_row = lax.broadcasted_iota(jnp.int32, (bq, 2 * bq), 0)
    q_col = lax.broadcasted_iota(jnp.int32, (bq, 2 * bq), 1)
    keep = key_row <= jnp.where(q_col >= bq, q_col - bq, q_col)

    def fold(x):
        return x.reshape(bq // SUBLANES, SUBLANES, 2 * bq)

    def qblock(i, carry):
        qs = pl.multiple_of(i * bq, bq)
        qt = q_ref[pl.ds(qs, bq), :].T
        zero = jnp.zeros_like(qt)
        q2t = jnp.concatenate([jnp.where(sub < D_HEAD, qt, zero), jnp.where(sub < D_HEAD, zero, qt)],
                              axis=1).astype(BF16)

        def scores(j):
            ks = pl.multiple_of(j * bq, bq)
            return ks, jnp.dot(kb_ref[pl.ds(ks, bq), :], q2t, preferred_element_type=F32)

        def pass_a(j, m_run):
            ks, s = scores(j)
            st_ref[pl.ds(ks, bq), :] = s
            return jnp.maximum(m_run, jnp.max(fold(s), axis=0))

        m_run = lax.fori_loop(0, i, pass_a, jnp.full((SUBLANES, 2 * bq), -jnp.inf, F32))
        _, s = scores(i)
        s = jnp.where(keep, s, -jnp.inf)
        st_ref[pl.ds(qs, bq), :] = s
        m_run = jnp.maximum(m_run, jnp.max(fold(s), axis=0))
        m = jnp.max(m_run, axis=0, keepdims=True)

        acc_ref[...] = jnp.zeros(acc_ref.shape, F32)
        l_ref[...] = jnp.zeros(l_ref.shape, F32)

        def pass_b(j, c):
            ks = pl.multiple_of(j * bq, bq)
            p = jnp.exp2(st_ref[pl.ds(ks, bq), :] - m)
            l_ref[...] += jnp.sum(fold(p), axis=0)
            acc_ref[...] += jnp.dot(vt_ref[j], p.astype(BF16), preferred_element_type=F32)
            return c

        lax.fori_loop(0, i + 1, pass_b, 0)
        l = jnp.sum(l_ref[...], axis=0, keepdims=True)
        d = acc_ref[...] * (1.0 / l)
        o = (d[:, :bq] - lam * d[:, bq:]).T
        o_ref[pl.ds(qs, bq), :] = _head_norm(o, g_ref[...], lam_init).astype(o_ref.dtype)
        return carry

    lax.fori_loop(0, nq, qblock, 0)


def attn_prompt(q, k_all, v_all, layer, lambda_qk, subln_g, lam_init, batch, seq, bq=512):
    m, width = q.shape
    nh = width // HEAD_W
    kv_spec = pl.BlockSpec((None, seq, HEAD_W), lambda b, h: (layer, b, h))
    return pl.pallas_call(
        functools.partial(_attn_prompt_body, bq=bq, lam_init=lam_init),
        grid=(batch, nh),
        in_specs=[pl.BlockSpec((seq, HEAD_W), lambda b, h: (b, h)), kv_spec, kv_spec,
                  pl.BlockSpec((4, D_HEAD), lambda b, h: (0, 0)),
                  pl.BlockSpec((1, HEAD_W), lambda b, h: (0, 0))],
        out_specs=pl.BlockSpec((seq, HEAD_W), lambda b, h: (b, h)),
        out_shape=jax.ShapeDtypeStruct((m, width), BF16),
        scratch_shapes=[pltpu.VMEM((seq, HEAD_W), BF16),
                        pltpu.VMEM((seq // bq, HEAD_W, bq), BF16),
                        pltpu.VMEM((seq, 2 * bq), F32),
                        pltpu.VMEM((HEAD_W, 2 * bq), F32),
                        pltpu.VMEM((SUBLANES, 2 * bq), F32)],
        compiler_params=_cparams(("parallel", "parallel")),
        name="attn_prompt",
    )(q, k_all, v_all, lambda_qk, subln_g.reshape(1, HEAD_W))


def _attn_sample_body(pt_ref, q_ref, bias_ref, *refs, n_steps, npp, nh, n_new, lam_init):
    kc_refs, vc_refs = refs[:npp], refs[npp:2 * npp]
    kn_ref, vn_ref, lq_ref, g_ref, o_ref, m_ref, l_ref, acc_ref = refs[2 * npp:]
    p = pl.program_id(1)
    rows = q_ref.shape[0]
    q = q_ref[...].astype(BF16)

    @pl.when(p == 0)
    def _():
        m_ref[...] = jnp.full(m_ref.shape, -jnp.inf, F32)
        l_ref[...] = jnp.zeros(l_ref.shape, F32)
        acc_ref[...] = jnp.zeros(acc_ref.shape, F32)

    def update(r0, nr, kv_pairs, biases):
        qg = q[r0:r0 + nr]
        s_list = [lax.dot_general(qg, kf.astype(BF16), (((1,), (1,)), ((), ())), preferred_element_type=F32) + b
                  for (kf, _), b in zip(kv_pairs, biases)]
        m_old = m_ref[r0:r0 + nr]
        s_max = functools.reduce(jnp.maximum, [jnp.max(s, axis=-1, keepdims=True) for s in s_list])
        m_new = jnp.maximum(m_old, s_max)
        alpha = jnp.exp2(m_old - m_new)
        l_new = alpha * l_ref[r0:r0 + nr]
        acc = alpha * acc_ref[r0:r0 + nr]
        for s, (_, vf) in zip(s_list, kv_pairs):
            pr = jnp.exp2(s - jnp.concatenate([m_new] * (s.shape[1] // LANES), axis=1))
            l_new = l_new + jnp.sum(pr, axis=-1, keepdims=True)
            acc = acc + jnp.dot(pr.astype(BF16), vf.astype(BF16), preferred_element_type=F32)
        m_ref[r0:r0 + nr] = m_new
        l_ref[r0:r0 + nr] = l_new
        acc_ref[r0:r0 + nr] = acc

    @pl.when(p < n_steps)
    def _():
        ntok = kc_refs[0].shape[0]
        bias = bias_ref[...]
        for hg in range(nh // SUBLANES):
            hs = slice(hg * SUBLANES, (hg + 1) * SUBLANES)
            update(hg * 8 * SUBLANES, 8 * SUBLANES,
                   [(kr[:, hs, :].reshape(ntok * SUBLANES, HEAD_W), vr[:, hs, :].reshape(ntok * SUBLANES, HEAD_W))
                    for kr, vr in zip(kc_refs, vc_refs)], [bias] * npp)

    @pl.when(p == n_steps)
    def _():
        cols = kn_ref.shape[0]
        r = lax.broadcasted_iota(jnp.int32, (rows, cols), 0)
        c = lax.broadcasted_iota(jnp.int32, (rows, cols), 1)
        keep = jnp.logical_and((c % nh) == (r // 8), (c // nh) <= (r % n_new))
        update(0, rows, [(kn_ref[...], vn_ref[...])], [jnp.where(keep, 0.0, -jnp.inf)])
        lam = _lam_value(lq_ref, lam_init)
        d = acc_ref[...] / l_ref[...]
        o = d - lam * pltpu.roll(d, rows - n_new, axis=0)
        o_ref[...] = _head_norm(o, g_ref[...], lam_init).astype(o_ref.dtype)


def attn_sample(qrows, cache_k, cache_v, knew, vnew, page_table, lambda_qk, subln_g, lam_init, layer, n_new):
    nb, rows, _ = qrows.shape
    nh = rows // 8
    assert 2 * n_new == 8
    n_pages = page_table.shape[1]
    page = cache_k.shape[2]
    npp = PAGES_PER_STEP
    assert n_pages % npp == 0
    n_steps = n_pages // npp
    assert nh % SUBLANES == 0
    brow, bcol = 8 * SUBLANES, page * SUBLANES
    bias = jnp.where((jnp.arange(bcol)[None, :] % SUBLANES) == (jnp.arange(brow)[:, None] // 8),
                     0.0, -jnp.inf).astype(F32)

    def cache_spec(i):
        def cache_map(b, p, pt):
            return (layer, pt[b, jnp.minimum(p, n_steps - 1) * npp + i], 0, 0, 0)
        return pl.BlockSpec((None, None, page, nh, HEAD_W), cache_map)

    per_b = lambda shape: pl.BlockSpec((None,) + shape, lambda b, p, pt: (b, 0, 0))
    const = lambda shape: pl.BlockSpec(shape, lambda b, p, pt: (0, 0))
    grid_spec = pltpu.PrefetchScalarGridSpec(
        num_scalar_prefetch=1,
        grid=(nb, n_steps + 1),
        in_specs=[per_b((rows, HEAD_W)), const((brow, bcol))]
                 + [cache_spec(i) for i in range(npp)] * 2
                 + [per_b((8 * nh, HEAD_W)), per_b((8 * nh, HEAD_W)), const((4, D_HEAD)), const((1, HEAD_W))],
        out_specs=per_b((rows, HEAD_W)),
        scratch_shapes=[pltpu.VMEM((rows, LANES), F32), pltpu.VMEM((rows, LANES), F32),
                        pltpu.VMEM((rows, HEAD_W), F32)],
    )
    return pl.pallas_call(
        functools.partial(_attn_sample_body, n_steps=n_steps, npp=npp, nh=nh, n_new=n_new, lam_init=lam_init),
        grid_spec=grid_spec,
        out_shape=jax.ShapeDtypeStruct((nb, rows, HEAD_W), F32),
        compiler_params=_cparams(("parallel", "arbitrary")),
        name="attn_sample",
    )(page_table, qrows, bias, *([cache_k] * npp), *([cache_v] * npp), knew, vnew, lambda_qk,
      subln_g.reshape(1, HEAD_W))


def _split3(x):
    hi = x.astype(BF16)
    r1 = x - hi.astype(F32)
    mid = r1.astype(BF16)
    lo = (r1 - mid.astype(F32)).astype(BF16)
    return hi, mid, lo


def _silu(x):
    return x * (1.0 / (1.0 + jnp.exp(-x)))


def _ssd_body(xbc_ref, z_ref, dtr_ref, cprev_ref, h0_ref, cw_ref, cb_ref, dtb_ref, alog_ref,
              dskip_ref, ng_ref, e64_ref, et_ref, y_ref, hout_ref,
              tail_ref, st_ref, xpad_ref, *, valid, nheads):
    c = pl.program_id(1)
    nc = pl.num_programs(1)
    T = xbc_ref.shape[0]
    width = z_ref.shape[1]
    gn = SSM_GROUPS * SSM_STATE
    hpg = nheads // SSM_GROUPS
    gw = hpg * SSM_HEAD_DIM
    ntile = width // LANES

    @pl.when(c == 0)
    def _():
        tail_ref[...] = cprev_ref[...]
        h0 = h0_ref[...].reshape(width, SSM_STATE)
        for j in range(ntile):
            st_ref[:, j * LANES:(j + 1) * LANES] = h0[j * LANES:(j + 1) * LANES, :].T

    xbc = xbc_ref[...]
    xpad_ref[0:SUBLANES, :] = tail_ref[...]
    xpad_ref[SUBLANES:SUBLANES + T, :] = xbc
    tail_ref[...] = xbc[T - SUBLANES:, :]
    acc = cb_ref[...] + xbc * cw_ref[CONV_K - 1:CONV_K, :]
    for j in range(CONV_K - 1):
        off = SUBLANES - (CONV_K - 1) + j
        acc = acc + xpad_ref[off:off + T, :] * cw_ref[j:j + 1, :]
    xc = _silu(acc)
    xs = xc[:, :width]
    bm = xc[:, width:width + gn].astype(BF16)
    cm = xc[:, width + gn:].astype(BF16)

    lane = lax.broadcasted_iota(jnp.int32, (T, LANES), 1)
    row = lax.broadcasted_iota(jnp.int32, (T, LANES), 0)
    live = jnp.logical_and(lane < nheads, row < valid)
    xdt = dtr_ref[...] + dtb_ref[...]
    dt = jnp.where(live, jnp.maximum(xdt, 0.0) + jnp.log1p(jnp.exp(-jnp.abs(xdt))), 0.0)
    da = dt * (-jnp.exp(alog_ref[...]))
    tr = lax.broadcasted_iota(jnp.int32, (T, T), 0)
    tc = lax.broadcasted_iota(jnp.int32, (T, T), 1)
    causal = tr >= tc
    ltri = jnp.where(causal, 1.0, 0.0).astype(BF16)
    cs = jnp.dot(jnp.concatenate([ltri, ltri, ltri], axis=1),
                 jnp.concatenate(_split3(da), axis=0), preferred_element_type=F32)
    cs_t = cs.T

    stack = jnp.concatenate([dt, cs], axis=0)
    exp64 = jnp.dot(jnp.concatenate(_split3(stack), axis=1), e64_ref[...], preferred_element_type=F32)
    dt_x = exp64[:T]
    cs_x = exp64[T:]
    cs_b = jnp.dot(jnp.concatenate(_split3(cs), axis=1), et_ref[...], preferred_element_type=F32)

    cs_last = cs_x[T - 1:T, :]
    ecs_x = jnp.exp(cs_x)
    dte_x = jnp.exp(cs_last - cs_x)
    x = xs * dt_x
    xb = x.astype(BF16)
    xdb = (x * dte_x).astype(BF16)

    lane2 = lax.broadcasted_iota(jnp.int32, (T, LANES), 1)
    y_parts = []
    for g in range(SSM_GROUPS):
        bg = bm[:, g * SSM_STATE:(g + 1) * SSM_STATE]
        cg = cm[:, g * SSM_STATE:(g + 1) * SSM_STATE]
        cb = lax.dot_general(cg, bg, (((1,), (1,)), ((), ())), preferred_element_type=F32)
        st_g = st_ref[:, g * gw:(g + 1) * gw]
        y_off = jnp.dot(cg, st_g.astype(BF16), preferred_element_type=F32) * ecs_x[:, g * gw:(g + 1) * gw]
        y_diag = []
        for pr in range(gw // LANES):
            mats = []
            for hh in range(LANES // SSM_HEAD_DIM):
                h = g * hpg + pr * (LANES // SSM_HEAD_DIM) + hh
                seg = cs_b[:, h * T:(h + 1) * T] - cs_t[h:h + 1, :]
                lmat = jnp.exp(jnp.where(causal, seg, -jnp.inf))
                mats.append((cb * lmat).astype(BF16))
            tile = g * (gw // LANES) + pr
            x2 = xb[:, tile * LANES:(tile + 1) * LANES]
            zero = jnp.zeros_like(x2)
            xbd = jnp.concatenate([jnp.where(lane2 < SSM_HEAD_DIM, x2, zero),
                                   jnp.where(lane2 < SSM_HEAD_DIM, zero, x2)], axis=0)
            y_diag.append(jnp.dot(jnp.concatenate(mats, axis=1), xbd, preferred_element_type=F32))
        y_parts.append(jnp.concatenate(y_diag, axis=1) + y_off)
        bg_t = xc[:, width + g * SSM_STATE:width + (g + 1) * SSM_STATE].T.astype(BF16)
        new_states = jnp.dot(bg_t, xdb[:, g * gw:(g + 1) * gw], preferred_element_type=F32)
        decay = ecs_x[T - 1:T, g * gw:(g + 1) * gw]
        st_ref[:, g * gw:(g + 1) * gw] = st_g * decay + new_states

    y = jnp.concatenate(y_parts, axis=1) + dskip_ref[...] * xs
    y = y * _silu(z_ref[...])
    outs = []
    for g in range(SSM_GROUPS):
        yg = y[:, g * gw:(g + 1) * gw]
        ms = jnp.mean(yg * yg, axis=-1, keepdims=True)
        outs.append(yg * lax.rsqrt(ms + NORM_EPS))
    y_ref[...] = (jnp.concatenate(outs, axis=1) * ng_ref[...]).astype(y_ref.dtype)

    @pl.when(c == nc - 1)
    def _():
        for j in range(ntile):
            hout_ref[j * LANES // SSM_HEAD_DIM:(j + 1) * LANES // SSM_HEAD_DIM] = (
                st_ref[:, j * LANES:(j + 1) * LANES].T.reshape(LANES // SSM_HEAD_DIM, SSM_HEAD_DIM, SSM_STATE))


def ssd_mixer(xbc, z, dtr, conv_prev, h0, conv_w, conv_b, dt_bias, a_log, d_skip, norm_g,
              batch, valid):
    mtot, conv_dim = xbc.shape
    width = z.shape[1]
    nheads = width // SSM_HEAD_DIM
    T = SSD_CHUNK
    nc = mtot // batch // T
    hid = jnp.arange(LANES)[:, None]
    e64 = (jnp.arange(width)[None, :] // SSM_HEAD_DIM == hid).astype(BF16)
    et = (jnp.arange(nheads * T)[None, :] // T == hid).astype(BF16)
    e64 = jnp.concatenate([e64] * 3, axis=0)
    et = jnp.concatenate([et] * 3, axis=0)
    pad_l = lambda v: jnp.pad(v.astype(F32), (0, LANES - nheads)).reshape(1, LANES)
    cw = jnp.pad(conv_w, ((0, SUBLANES - CONV_K), (0, 0)))
    full = lambda shape: pl.BlockSpec(shape, lambda b, c: (0,) * len(shape))
    y, hout = pl.pallas_call(
        functools.partial(_ssd_body, valid=valid, nheads=nheads),
        grid=(batch, nc),
        in_specs=[pl.BlockSpec((T, conv_dim), lambda b, c: (b * nc + c, 0)),
                  pl.BlockSpec((T, width), lambda b, c: (b * nc + c, 0)),
                  pl.BlockSpec((T, LANES), lambda b, c: (b * nc + c, 0)),
                  pl.BlockSpec((None, SUBLANES, conv_dim), lambda b, c: (b, 0, 0)),
                  pl.BlockSpec((None, nheads, SSM_HEAD_DIM, SSM_STATE), lambda b, c: (b, 0, 0, 0)),
                  full((SUBLANES, conv_dim)), full((1, conv_dim)), full((1, LANES)), full((1, LANES)),
                  full((1, width)), full((1, width)), full((3 * LANES, width)), full((3 * LANES, nheads * T))],
        out_specs=[pl.BlockSpec((T, width), lambda b, c: (b * nc + c, 0)),
                   pl.BlockSpec((None, nheads, SSM_HEAD_DIM, SSM_STATE), lambda b, c: (b, 0, 0, 0))],
        out_shape=[jax.ShapeDtypeStruct((mtot, width), BF16),
                   jax.ShapeDtypeStruct((batch, nheads, SSM_HEAD_DIM, SSM_STATE), F32)],
        scratch_shapes=[pltpu.VMEM((SUBLANES, conv_dim), F32),
                        pltpu.VMEM((SSM_STATE, width), F32),
                        pltpu.VMEM((SUBLANES + T, conv_dim), F32)],
        compiler_params=_cparams(("parallel", "arbitrary")),
        name="ssd_mixer",
    )(xbc, z, dtr, conv_prev, h0, cw, conv_b.reshape(1, conv_dim), pad_l(dt_bias), pad_l(a_log),
      jnp.repeat(d_skip.astype(F32), SSM_HEAD_DIM).reshape(1, width), norm_g.reshape(1, width), e64, et)
    return y, hout


def _layer(x_p, x_s, l, lam_init, dims, cache_k, cache_v, state_conv, state_ssm, page_table, w, kv_bufs):
    bp, seq, db, dseq = dims
    att_w, ssm_w, conv_dim, nheads = w["att_w"], w["ssm_w"], w["conv_dim"], w["nheads"]
    nh = att_w // HEAD_W
    offs = {"q": 0, "k": att_w, "v": 2 * att_w, "z": 3 * att_w, "xbc": 3 * att_w + ssm_w}
    ssm_args = (w["conv_w"][l], w["conv_b"][l], w["dt_bias"][l], w["a_log"][l], w["d_skip"][l],
                w["ssm_norm_g"][l])

    u_p = rmsnorm(x_p, w["attn_norm_g"][l], BF16)
    u_s = rmsnorm(x_s, w["attn_norm_g"][l], BF16)
    mm = functools.partial(matmul, u_p, u_s, w["w_in_t"], l, w_t=True)
    q_p, q_s = mm(n_off=offs["q"], n=att_w, scale=D_HEAD ** -0.5 * math.log2(math.e))
    k_all, k_s = mm(n_off=offs["k"], n=att_w, stacked=True, out_buf=kv_bufs[0])
    v_all, v_s = mm(n_off=offs["v"], n=att_w, stacked=True, out_buf=kv_bufs[1])
    z_p, z_s = mm(n_off=offs["z"], n=ssm_w)
    xbc_p, xbc_s = mm(n_off=offs["xbc"], n=conv_dim)
    dtr_p, dtr_s = matmul(u_p, u_s, w["w_dt_t"], l, w_t=True)

    o_p = attn_prompt(q_p, k_all, v_all, l, w["lambda_qk"][l], w["subln_g"][l], lam_init, bp, seq)
    y_p, ssm_p = ssd_mixer(xbc_p, z_p, dtr_p, jnp.zeros((bp, SUBLANES, conv_dim), F32),
                           jnp.zeros((bp, nheads, SSM_HEAD_DIM, SSM_STATE), F32), *ssm_args,
                           batch=bp, valid=SSD_CHUNK)
    conv_p = xbc_p.reshape(bp, seq, conv_dim)[:, seq - (CONV_K - 1):]

    q5 = q_s.reshape(db, dseq, nh, 2, D_HEAD).transpose(0, 2, 1, 3, 4)
    sel = jnp.eye(2, dtype=F32)[None, None, :, None, :, None]
    qrows = (q5[:, :, None] * sel).reshape(db, nh * 8, HEAD_W)

    def tok_rows(t):
        t = t.reshape(db, dseq, nh, HEAD_W)
        return jnp.pad(t, ((0, 0), (0, SUBLANES - dseq), (0, 0), (0, 0))).reshape(db, SUBLANES * nh, HEAD_W)

    o_s = attn_sample(qrows, cache_k, cache_v, tok_rows(k_s), tok_rows(v_s), page_table,
                      w["lambda_qk"][l], w["subln_g"][l], lam_init, l, dseq)
    o_s = o_s.reshape(db, nh, 8, HEAD_W)[:, :, :dseq].transpose(0, 2, 1, 3).reshape(db * dseq, att_w)

    def pad_chunk(t):
        t = t.reshape(db, dseq, t.shape[-1])
        return jnp.pad(t, ((0, 0), (0, SSD_CHUNK - dseq), (0, 0))).reshape(db * SSD_CHUNK, t.shape[-1])

    cprev = jnp.pad(state_conv[l], ((0, 0), (SUBLANES - (CONV_K - 1), 0), (0, 0)))
    y_s, ssm_s = ssd_mixer(pad_chunk(xbc_s), pad_chunk(z_s), pad_chunk(dtr_s), cprev, state_ssm[l], *ssm_args,
                           batch=db, valid=dseq)
    y_s = y_s.reshape(db, SSD_CHUNK, ssm_w)[:, :dseq].reshape(db * dseq, ssm_w)
    conv_s = jnp.concatenate([state_conv[l], xbc_s.reshape(db, dseq, conv_dim)], axis=1)[:, -(CONV_K - 1):]

    h_p, h_s = matmul((o_p, y_p), (o_s.astype(BF16), y_s), w["w_out"], l, res=(x_p, x_s))
    m_p = rmsnorm(h_p, w["mlp_norm_g"][l], BF16)
    m_s = rmsnorm(h_s, w["mlp_norm_g"][l], BF16)
    a_p, a_s = matmul(m_p, m_s, w["w_up"], l, out_dtype=BF16, act="relu2")
    x_p, x_s = matmul(a_p, a_s, w["w_down"], l, res=(h_p, h_s), tk=2048)

    outs = (conv_p, ssm_p, k_s.reshape(db, dseq, nh, HEAD_W), v_s.reshape(db, dseq, nh, HEAD_W), conv_s, ssm_s)
    return x_p, x_s, (k_all, v_all), outs


def kernel(x_prompt, x_sample, cache_k, cache_v, state_conv, state_ssm, page_table, attn_norm_g, w_in,
           lambda_qk, subln_g, conv_w, conv_b, dt_bias, a_log, d_skip, ssm_norm_g, w_out, mlp_norm_g,
           w_up, w_down, final_norm_g):
    bp, seq, d_model = x_prompt.shape
    db, dseq, _ = x_sample.shape
    depth = w_in.shape[0]
    nheads = dt_bias.shape[1]
    ssm_w = ssm_norm_g.shape[1]
    att_w = d_model - ssm_w
    nh = att_w // HEAD_W
    conv_dim = conv_w.shape[2]
    main_cols = 3 * att_w + ssm_w + conv_dim
    w_in_t = jnp.swapaxes(w_in, 1, 2)
    w = dict(att_w=att_w, ssm_w=ssm_w, conv_dim=conv_dim, nheads=nheads,
             w_in_t=w_in_t, w_dt_t=jnp.pad(w_in_t[:, main_cols:, :], ((0, 0), (0, LANES - nheads), (0, 0))),
             w_out=w_out, w_up=w_up, w_down=w_down,
             attn_norm_g=attn_norm_g, mlp_norm_g=mlp_norm_g, lambda_qk=lambda_qk, subln_g=subln_g,
             conv_w=conv_w, conv_b=conv_b, dt_bias=dt_bias, a_log=a_log, d_skip=d_skip, ssm_norm_g=ssm_norm_g)
    x_p = x_prompt.reshape(bp * seq, d_model)
    x_s = x_sample.reshape(db * dseq, d_model)
    kv_bufs = (jnp.zeros((depth, bp * seq, att_w), F32), jnp.zeros((depth, bp * seq, att_w), F32))
    per_layer = []
    for l in range(depth):
        lam_init = 0.8 - 0.6 * math.exp(-0.3 * l)
        x_p, x_s, kv_bufs, outs = _layer(x_p, x_s, l, lam_init, (bp, seq, db, dseq), cache_k, cache_v,
                                         state_conv, state_ssm, page_table, w, kv_bufs)
        per_layer.append(outs)
    y_p = rmsnorm(x_p, final_norm_g, F32).reshape(bp, seq, d_model)
    y_s = rmsnorm(x_s, final_norm_g, F32).reshape(db, dseq, d_model)
    conv_p, ssm_p, k_s, v_s, conv_s, ssm_s = [jnp.stack([per_layer[l][i] for l in range(depth)])
                                              for i in range(6)]
    k_p = kv_bufs[0].reshape(depth, bp, seq, nh, HEAD_W)
    v_p = kv_bufs[1].reshape(depth, bp, seq, nh, HEAD_W)
    return (y_p, y_s, k_p, v_p, conv_p, ssm_p, k_s, v_s, conv_s, ssm_s)
```

```python
import functools
import math

import jax
import jax.numpy as jnp
from jax import lax
from jax.experimental import pallas as pl
from jax.experimental.pallas import tpu as pltpu

D_HEAD = 64
HEAD_W = 2 * D_HEAD
SSM_HEAD_DIM = 64
SSM_GROUPS = 8
SSM_STATE = 128
CONV_K = 4
SSD_CHUNK = 128
NORM_EPS = 1e-6
LANES = 128
SUBLANES = 8
VMEM_LIMIT = 56 * 1024 * 1024
PAGES_PER_STEP = 8

F32 = jnp.float32
BF16 = jnp.bfloat16


def _cparams(sem):
    return pltpu.CompilerParams(dimension_semantics=sem, vmem_limit_bytes=VMEM_LIMIT)


def _rmsnorm_body(x_ref, g_ref, o_ref):
    x = x_ref[...]
    ms = jnp.mean(x * x, axis=-1, keepdims=True)
    o_ref[...] = (x * lax.rsqrt(ms + NORM_EPS) * g_ref[...]).astype(o_ref.dtype)


def rmsnorm(x, g, out_dtype, rows=256):
    m, d = x.shape
    rows = min(rows, m)
    return pl.pallas_call(
        _rmsnorm_body,
        grid=(m // rows,),
        in_specs=[pl.BlockSpec((rows, d), lambda i: (i, 0)),
                  pl.BlockSpec((1, d), lambda i: (0, 0))],
        out_specs=pl.BlockSpec((rows, d), lambda i: (i, 0)),
        out_shape=jax.ShapeDtypeStruct((m, d), out_dtype),
        compiler_params=_cparams(("parallel",)),
        name="rmsnorm",
    )(x, g.reshape(1, d))


def _mm_body(*refs, n_a, nk, kdim, act, scale, has_res, has_buf, has_ss, has_norm, w_t):
    ap_refs, as_refs, w_ref = refs[:n_a], refs[n_a:2 * n_a], refs[2 * n_a]
    pos = 2 * n_a + 1
    rp_ref = rs_ref = ssp_ref = sss_ref = g_ref = None
    if has_res:
        rp_ref, rs_ref = refs[pos:pos + 2]
        pos += 2
    if has_ss:
        ssp_ref, sss_ref = refs[pos:pos + 2]
        pos += 2
    if has_norm:
        g_ref = refs[pos]
        pos += 1
    pos += int(has_buf)
    op_ref, os_ref = refs[pos:pos + 2]
    pos += 2
    norm_p = norm_s = None
    if has_norm:
        norm_p, norm_s = (refs[pos], refs[pos + 2]), (refs[pos + 1], refs[pos + 3])
        pos += 4
    wb_ref = refs[pos]
    scr = refs[pos + 1:]
    acc_refs = scr[:2] if nk > 1 else (None, None)
    ssacc_refs = scr[-2:] if has_norm else (None, None)
    j, k, i = pl.program_id(0), pl.program_id(1), pl.program_id(2)

    @pl.when(i == 0)
    def _():
        if w_t:
            for c in range(w_ref.shape[0] // LANES):
                wb_ref[:, c * LANES:(c + 1) * LANES] = w_ref[c * LANES:(c + 1) * LANES, :].T.astype(BF16)
        else:
            wb_ref[...] = w_ref[...].astype(BF16)

    def lhs_dot(a_refs):
        off, tot = 0, None
        for a_ref in a_refs:
            kk = a_ref.shape[1]
            d = jnp.dot(a_ref[...], wb_ref[off:off + kk, :], preferred_element_type=F32)
            tot = d if tot is None else tot + d
            off += kk
        return tot

    def finish(acc, r_ref, o_ref, ss_ref, norm, ssacc_ref):
        tn = acc.shape[1]
        if ss_ref is not None:
            rinv = lax.rsqrt(ss_ref[...] * (1.0 / kdim) + NORM_EPS)
            acc = acc * jnp.concatenate([rinv] * (tn // LANES), axis=1)
        if act == "relu2":
            acc = jnp.square(jnp.maximum(acc, 0.0))
        if scale != 1.0:
            acc = acc * scale
        if r_ref is not None:
            acc = acc + r_ref[...]
        o_ref[...] = acc.astype(o_ref.dtype)
        if norm is not None:
            hg_ref, sso_ref = norm
            hg_ref[...] = (acc * g_ref[...]).astype(hg_ref.dtype)
            part = jnp.broadcast_to(jnp.sum(acc * acc, axis=-1, keepdims=True), ssacc_ref.shape)

            @pl.when(j == 0)
            def _():
                ssacc_ref[...] = part

            @pl.when(j > 0)
            def _():
                ssacc_ref[...] += part

            sso_ref[...] = ssacc_ref[...]

    def accumulate(a_refs, acc_ref, *fin):
        if nk == 1:
            finish(lhs_dot(a_refs), *fin)
            return

        @pl.when(k == 0)
        def _():
            acc_ref[...] = lhs_dot(a_refs)

        @pl.when(jnp.logical_and(k > 0, k < nk - 1))
        def _():
            acc_ref[...] = acc_ref[...] + lhs_dot(a_refs)

        @pl.when(k == nk - 1)
        def _():
            finish(acc_ref[...] + lhs_dot(a_refs), *fin)

    accumulate(ap_refs, acc_refs[0].at[i] if nk > 1 else None, rp_ref, op_ref, ssp_ref, norm_p,
               ssacc_refs[0].at[i] if has_norm else None)

    @pl.when(i == 0)
    def _():
        accumulate(as_refs, acc_refs[1], rs_ref, os_ref, sss_ref, norm_s, ssacc_refs[1])


def matmul(a_p, a_s, w, layer, *, n_off=0, n=None, out_dtype=F32, act=None, scale=1.0, res=None, out_buf=None,
           stacked=False, w_t=False, row_ss=None, norm_g=None, tm=1024, tn=512, tk=4096):
    ap_list = a_p if isinstance(a_p, tuple) else (a_p,)
    as_list = a_s if isinstance(a_s, tuple) else (a_s,)
    n_a = len(ap_list)
    mp, ms = ap_list[0].shape[0], as_list[0].shape[0]
    kdim = sum(x.shape[1] for x in ap_list)
    n = w.shape[1 if w_t else 2] - n_off if n is None else n
    tm, tn = min(tm, mp), min(tn, n)
    tk = kdim if n_a > 1 else min(tk, kdim)
    assert mp % tm == 0 and n % tn == 0 and kdim % tk == 0 and n_off % tn == 0
    ni, nj, nk = mp // tm, n // tn, kdim // tk
    joff = n_off // tn
    row = (lambda i, k: i) if nk == 1 else (lambda i, k: jnp.where(k == nk - 1, i, 0))
    if n_a > 1:
        in_specs = [pl.BlockSpec((tm, x.shape[1]), lambda j, k, i: (i, 0)) for x in ap_list]
        in_specs += [pl.BlockSpec((ms, x.shape[1]), lambda j, k, i: (0, 0)) for x in as_list]
    else:
        in_specs = [pl.BlockSpec((tm, tk), lambda j, k, i: (i, k)),
                    pl.BlockSpec((ms, tk), lambda j, k, i: (0, k))]
    def w_tile(j, k, i):
        nxt = jnp.minimum(j * nk + k + jnp.minimum(i, 1), nj * nk - 1)
        return nxt // nk, nxt % nk

    if w_t:
        in_specs.append(pl.BlockSpec((None, tn, tk), lambda j, k, i: (layer, w_tile(j, k, i)[0] + joff,
                                                                      w_tile(j, k, i)[1])))
    else:
        in_specs.append(pl.BlockSpec((None, tk, tn), lambda j, k, i: (layer, w_tile(j, k, i)[1],
                                                                      w_tile(j, k, i)[0] + joff)))
    args = list(ap_list) + list(as_list) + [w]
    if res is not None:
        in_specs += [pl.BlockSpec((tm, tn), lambda j, k, i: (row(i, k), j)),
                     pl.BlockSpec((ms, tn), lambda j, k, i: (0, j))]
        args += list(res)
    if row_ss is not None:
        in_specs += [pl.BlockSpec((tm, LANES), lambda j, k, i: (row(i, k), 0)),
                     pl.BlockSpec((ms, LANES), lambda j, k, i: (0, 0))]
        args += list(row_ss)
    if norm_g is not None:
        in_specs.append(pl.BlockSpec((1, tn), lambda j, k, i: (0, j)))
        args.append(norm_g.reshape(1, n))
    aliases = {}
    if stacked:
        if out_buf is not None:
            in_specs.append(pl.BlockSpec(memory_space=pl.ANY))
            aliases = {len(args): 0}
            args.append(out_buf)
        out_p_spec = pl.BlockSpec((None, tm, tn), lambda j, k, i: (layer, row(i, k), j))
        out_p_shape = jax.ShapeDtypeStruct((w.shape[0], mp, n), out_dtype)
    else:
        out_p_spec = pl.BlockSpec((tm, tn), lambda j, k, i: (row(i, k), j))
        out_p_shape = jax.ShapeDtypeStruct((mp, n), out_dtype)
    out_specs = [out_p_spec, pl.BlockSpec((ms, tn), lambda j, k, i: (0, j))]
    out_shape = [out_p_shape, jax.ShapeDtypeStruct((ms, n), out_dtype)]
    scratch = [pltpu.VMEM((tk, tn), BF16)]
    if nk > 1:
        scratch += [pltpu.VMEM((ni, tm, tn), F32), pltpu.VMEM((ms, tn), F32)]
    if norm_g is not None:
        out_specs += [pl.BlockSpec((tm, tn), lambda j, k, i: (row(i, k), j)),
                      pl.BlockSpec((ms, tn), lambda j, k, i: (0, j)),
                      pl.BlockSpec((tm, LANES), lambda j, k, i: (jnp.where(j == nj - 1, row(i, k), 0), 0)),
                      pl.BlockSpec((ms, LANES), lambda j, k, i: (0, 0))]
        out_shape += [jax.ShapeDtypeStruct((mp, n), BF16), jax.ShapeDtypeStruct((ms, n), BF16),
                      jax.ShapeDtypeStruct((mp, LANES), F32), jax.ShapeDtypeStruct((ms, LANES), F32)]
        scratch += [pltpu.VMEM((ni, tm, LANES), F32), pltpu.VMEM((ms, LANES), F32)]
    col_sem = "parallel" if norm_g is None else "arbitrary"
    return pl.pallas_call(
        functools.partial(_mm_body, n_a=n_a, nk=nk, kdim=kdim, act=act, scale=scale,
                          has_res=res is not None, has_buf=out_buf is not None, has_ss=row_ss is not None,
                          has_norm=norm_g is not None, w_t=w_t),
        grid=(nj, nk, ni),
        in_specs=in_specs,
        out_specs=out_specs,
        out_shape=out_shape,
        scratch_shapes=scratch,
        input_output_aliases=aliases,
        compiler_params=_cparams((col_sem, "arbitrary", "arbitrary")),
        name="matmul",
    )(*args)


def _lam_value(lq_ref, lam_init):
    lq = lq_ref[...]
    a = jnp.sum(lq[0:1] * lq[1:2], axis=-1, keepdims=True)
    b = jnp.sum(lq[2:3] * lq[3:4], axis=-1, keepdims=True)
    return jnp.exp(a) - jnp.exp(b) + lam_init


def _head_norm(o, g, lam_init):
    ms = jnp.mean(o * o, axis=-1, keepdims=True)
    return o * lax.rsqrt(ms + NORM_EPS) * g * (1.0 - lam_init)


def _attn_prompt_body(q_ref, k_ref, v_ref, lq_ref, g_ref, o_ref, kb_ref, vt_ref, st_ref, acc_ref, l_ref,
                      *, bq, lam_init):
    seq = q_ref.shape[0]
    nq = seq // bq
    kb_ref[...] = k_ref[...].astype(BF16)
    for j in range(nq):
        vt_ref[j] = v_ref[j * bq:(j + 1) * bq, :].T.astype(BF16)
    lam = _lam_value(lq_ref, lam_init)
    sub = lax.broadcasted_iota(jnp.int32, (HEAD_W, bq), 0)
    key_row = lax.broadcasted_iota(jnp.int32, (bq, 2 * bq), 0)
    q_col = lax.broadcasted_iota(jnp.int32, (bq, 2 * bq), 1)
    keep = key_row <= jnp.where(q_col >= bq, q_col - bq, q_col)

    def fold(x):
        return x.reshape(bq // SUBLANES, SUBLANES, 2 * bq)

    def qblock(i, carry):
        qs = pl.multiple_of(i * bq, bq)
        qt = q_ref[pl.ds(qs, bq), :].T
        zero = jnp.zeros_like(qt)
        q2t = jnp.concatenate([jnp.where(sub < D_HEAD, qt, zero), jnp.where(sub < D_HEAD, zero, qt)],
                              axis=1).astype(BF16)

        def scores(j):
            ks = pl.multiple_of(j * bq, bq)
            return ks, jnp.dot(kb_ref[pl.ds(ks, bq), :], q2t, preferred_element_type=F32)

        def pass_a(j, m_run):
            ks, s = scores(j)
            st_ref[pl.ds(ks, bq), :] = s
            return jnp.maximum(m_run, jnp.max(fold(s), axis=0))

        m_run = lax.fori_loop(0, i, pass_a, jnp.full((SUBLANES, 2 * bq), -jnp.inf, F32))
        _, s = scores(i)
        s = jnp.where(keep, s, -jnp.inf)
        st_ref[pl.ds(qs, bq), :] = s
        m_run = jnp.maximum(m_run, jnp.max(fold(s), axis=0))
        m = jnp.max(m_run, axis=0, keepdims=True)

        acc_ref[...] = jnp.zeros(acc_ref.shape, F32)
        l_ref[...] = jnp.zeros(l_ref.shape, F32)

        def pass_b(j, c):
            ks = pl.multiple_of(j * bq, bq)
            p = jnp.exp2(st_ref[pl.ds(ks, bq), :] - m)
            l_ref[...] += jnp.sum(fold(p), axis=0)
            acc_ref[...] += jnp.dot(vt_ref[j], p.astype(BF16), preferred_element_type=F32)
            return c

        lax.fori_loop(0, i + 1, pass_b, 0)
        l = jnp.sum(l_ref[...], axis=0, keepdims=True)
        d = acc_ref[...] * (1.0 / l)
        o = (d[:, :bq] - lam * d[:, bq:]).T
        o_ref[pl.ds(qs, bq), :] = _head_norm(o, g_ref[...], lam_init).astype(o_ref.dtype)
        return carry

    lax.fori_loop(0, nq, qblock, 0)


def attn_prompt(q, k_all, v_all, layer, lambda_qk, subln_g, lam_init, batch, seq, bq=512):
    m, width = q.shape
    nh = width // HEAD_W
    kv_spec = pl.BlockSpec((None, seq, HEAD_W), lambda b, h: (layer, b, h))
    return pl.pallas_call(
        functools.partial(_attn_prompt_body, bq=bq, lam_init=lam_init),
        grid=(batch, nh),
        in_specs=[pl.BlockSpec((seq, HEAD_W), lambda b, h: (b, h)), kv_spec, kv_spec,
                  pl.BlockSpec((4, D_HEAD), lambda b, h: (0, 0)),
                  pl.BlockSpec((1, HEAD_W), lambda b, h: (0, 0))],
        out_specs=pl.BlockSpec((seq, HEAD_W), lambda b, h: (b, h)),
        out_shape=jax.ShapeDtypeStruct((m, width), BF16),
        scratch_shapes=[pltpu.VMEM((seq, HEAD_W), BF16),
                        pltpu.VMEM((seq // bq, HEAD_W, bq), BF16),
                        pltpu.VMEM((seq, 2 * bq), F32),
                        pltpu.VMEM((HEAD_W, 2 * bq), F32),
                        pltpu.VMEM((SUBLANES, 2 * bq), F32)],
        compiler_params=_cparams(("parallel", "parallel")),
        name="attn_prompt",
    )(q, k_all, v_all, lambda_qk, subln_g.reshape(1, HEAD_W))


def _attn_sample_body(pt_ref, q_ref, bias_ref, *refs, n_steps, npp, nh, n_new, lam_init):
    kc_refs, vc_refs = refs[:npp], refs[npp:2 * npp]
    kn_ref, vn_ref, lq_ref, g_ref, o_ref, m_ref, l_ref, acc_ref = refs[2 * npp:]
    p = pl.program_id(1)
    rows = q_ref.shape[0]
    q = q_ref[...].astype(BF16)

    @pl.when(p == 0)
    def _():
        m_ref[...] = jnp.full(m_ref.shape, -jnp.inf, F32)
        l_ref[...] = jnp.zeros(l_ref.shape, F32)
        acc_ref[...] = jnp.zeros(acc_ref.shape, F32)

    def update(r0, nr, kv_pairs, biases):
        qg = q[r0:r0 + nr]
        s_list = [lax.dot_general(qg, kf.astype(BF16), (((1,), (1,)), ((), ())), preferred_element_type=F32) + b
                  for (kf, _), b in zip(kv_pairs, biases)]
        m_old = m_ref[r0:r0 + nr]
        s_max = functools.reduce(jnp.maximum, [jnp.max(s, axis=-1, keepdims=True) for s in s_list])
        m_new = jnp.maximum(m_old, s_max)
        alpha = jnp.exp2(m_old - m_new)
        l_new = alpha * l_ref[r0:r0 + nr]
        acc = alpha * acc_ref[r0:r0 + nr]
        for s, (_, vf) in zip(s_list, kv_pairs):
            pr = jnp.exp2(s - jnp.concatenate([m_new] * (s.shape[1] // LANES), axis=1))
            l_new = l_new + jnp.sum(pr, axis=-1, keepdims=True)
            acc = acc + jnp.dot(pr.astype(BF16), vf.astype(BF16), preferred_element_type=F32)
        m_ref[r0:r0 + nr] = m_new
        l_ref[r0:r0 + nr] = l_new
        acc_ref[r0:r0 + nr] = acc

    @pl.when(p < n_steps)
    def _():
        ntok = kc_refs[0].shape[0]
        bias = bias_ref[...]
        for hg in range(nh // SUBLANES):
            hs = slice(hg * SUBLANES, (hg + 1) * SUBLANES)
            update(hg * 8 * SUBLANES, 8 * SUBLANES,
                   [(kr[:, hs, :].reshape(ntok * SUBLANES, HEAD_W), vr[:, hs, :].reshape(ntok * SUBLANES, HEAD_W))
                    for kr, vr in zip(kc_refs, vc_refs)], [bias] * npp)

    @pl.when(p == n_steps)
    def _():
        cols = kn_ref.shape[0]
        r = lax.broadcasted_iota(jnp.int32, (rows, cols), 0)
        c = lax.broadcasted_iota(jnp.int32, (rows, cols), 1)
        keep = jnp.logical_and((c % nh) == (r // 8), (c // nh) <= (r % n_new))
        update(0, rows, [(kn_ref[...], vn_ref[...])], [jnp.where(keep, 0.0, -jnp.inf)])
        lam = _lam_value(lq_ref, lam_init)
        d = acc_ref[...] / l_ref[...]
        o = d - lam * pltpu.roll(d, rows - n_new, axis=0)
        o_ref[...] = _head_norm(o, g_ref[...], lam_init).astype(o_ref.dtype)


def attn_sample(qrows, cache_k, cache_v, knew, vnew, page_table, lambda_qk, subln_g, lam_init, layer, n_new):
    nb, rows, _ = qrows.shape
    nh = rows // 8
    assert 2 * n_new == 8
    n_pages = page_table.shape[1]
    page = cache_k.shape[2]
    npp = PAGES_PER_STEP
    assert n_pages % npp == 0
    n_steps = n_pages // npp
    assert nh % SUBLANES == 0
    brow, bcol = 8 * SUBLANES, page * SUBLANES
    bias = jnp.where((jnp.arange(bcol)[None, :] % SUBLANES) == (jnp.arange(brow)[:, None] // 8),
                     0.0, -jnp.inf).astype(F32)

    def cache_spec(i):
        def cache_map(b, p, pt):
            return (layer, pt[b, jnp.minimum(p, n_steps - 1) * npp + i], 0, 0, 0)
        return pl.BlockSpec((None, None, page, nh, HEAD_W), cache_map)

    per_b = lambda shape: pl.BlockSpec((None,) + shape, lambda b, p, pt: (b, 0, 0))
    const = lambda shape: pl.BlockSpec(shape, lambda b, p, pt: (0, 0))
    grid_spec = pltpu.PrefetchScalarGridSpec(
        num_scalar_prefetch=1,
        grid=(nb, n_steps + 1),
        in_specs=[per_b((rows, HEAD_W)), const((brow, bcol))]
                 + [cache_spec(i) for i in range(npp)] * 2
                 + [per_b((8 * nh, HEAD_W)), per_b((8 * nh, HEAD_W)), const((4, D_HEAD)), const((1, HEAD_W))],
        out_specs=per_b((rows, HEAD_W)),
        scratch_shapes=[pltpu.VMEM((rows, LANES), F32), pltpu.VMEM((rows, LANES), F32),
                        pltpu.VMEM((rows, HEAD_W), F32)],
    )
    return pl.pallas_call(
        functools.partial(_attn_sample_body, n_steps=n_steps, npp=npp, nh=nh, n_new=n_new, lam_init=lam_init),
        grid_spec=grid_spec,
        out_shape=jax.ShapeDtypeStruct((nb, rows, HEAD_W), F32),
        compiler_params=_cparams(("parallel", "arbitrary")),
        name="attn_sample",
    )(page_table, qrows, bias, *([cache_k] * npp), *([cache_v] * npp), knew, vnew, lambda_qk,
      subln_g.reshape(1, HEAD_W))


def _split3(x):
    hi = x.astype(BF16)
    r1 = x - hi.astype(F32)
    mid = r1.astype(BF16)
    lo = (r1 - mid.astype(F32)).astype(BF16)
    return hi, mid, lo


def _silu(x):
    return x * (1.0 / (1.0 + jnp.exp(-x)))


def _ssd_body(xbc_ref, z_ref, dtr_ref, cprev_ref, h0_ref, cw_ref, cb_ref, dtb_ref, alog_ref,
              dskip_ref, ng_ref, e64_ref, et_ref, y_ref, hout_ref,
              tail_ref, st_ref, xpad_ref, *, valid, nheads):
    c = pl.program_id(1)
    nc = pl.num_programs(1)
    T = xbc_ref.shape[0]
    width = z_ref.shape[1]
    gn = SSM_GROUPS * SSM_STATE
    hpg = nheads // SSM_GROUPS
    gw = hpg * SSM_HEAD_DIM
    ntile = width // LANES

    @pl.when(c == 0)
    def _():
        tail_ref[...] = cprev_ref[...]
        h0 = h0_ref[...].reshape(width, SSM_STATE)
        for j in range(ntile):
            st_ref[:, j * LANES:(j + 1) * LANES] = h0[j * LANES:(j + 1) * LANES, :].T

    xbc = xbc_ref[...]
    xpad_ref[0:SUBLANES, :] = tail_ref[...]
    xpad_ref[SUBLANES:SUBLANES + T, :] = xbc
    tail_ref[...] = xbc[T - SUBLANES:, :]
    acc = cb_ref[...] + xbc * cw_ref[CONV_K - 1:CONV_K, :]
    for j in range(CONV_K - 1):
        off = SUBLANES - (CONV_K - 1) + j
        acc = acc + xpad_ref[off:off + T, :] * cw_ref[j:j + 1, :]
    xc = _silu(acc)
    xs = xc[:, :width]
    bm = xc[:, width:width + gn].astype(BF16)
    cm = xc[:, width + gn:].astype(BF16)

    lane = lax.broadcasted_iota(jnp.int32, (T, LANES), 1)
    row = lax.broadcasted_iota(jnp.int32, (T, LANES), 0)
    live = jnp.logical_and(lane < nheads, row < valid)
    xdt = dtr_ref[...] + dtb_ref[...]
    dt = jnp.where(live, jnp.maximum(xdt, 0.0) + jnp.log1p(jnp.exp(-jnp.abs(xdt))), 0.0)
    da = dt * (-jnp.exp(alog_ref[...]))
    tr = lax.broadcasted_iota(jnp.int32, (T, T), 0)
    tc = lax.broadcasted_iota(jnp.int32, (T, T), 1)
    causal = tr >= tc
    ltri = jnp.where(causal, 1.0, 0.0).astype(BF16)
    cs = jnp.dot(jnp.concatenate([ltri, ltri, ltri], axis=1),
                 jnp.concatenate(_split3(da), axis=0), preferred_element_type=F32)
    cs_t = cs.T

    stack = jnp.concatenate([dt, cs], axis=0)
    exp64 = jnp.dot(jnp.concatenate(_split3(stack), axis=1), e64_ref[...], preferred_element_type=F32)
    dt_x = exp64[:T]
    cs_x = exp64[T:]
    cs_b = jnp.dot(jnp.concatenate(_split3(cs), axis=1), et_ref[...], preferred_element_type=F32)

    cs_last = cs_x[T - 1:T, :]
    ecs_x = jnp.exp(cs_x)
    dte_x = jnp.exp(cs_last - cs_x)
    x = xs * dt_x
    xb = x.astype(BF16)
    xdb = (x * dte_x).astype(BF16)

    lane2 = lax.broadcasted_iota(jnp.int32, (T, LANES), 1)
    y_parts = []
    for g in range(SSM_GROUPS):
        bg = bm[:, g * SSM_STATE:(g + 1) * SSM_STATE]
        cg = cm[:, g * SSM_STATE:(g + 1) * SSM_STATE]
        cb = lax.dot_general(cg, bg, (((1,), (1,)), ((), ())), preferred_element_type=F32)
        st_g = st_ref[:, g * gw:(g + 1) * gw]
        y_off = jnp.dot(cg, st_g.astype(BF16), preferred_element_type=F32) * ecs_x[:, g * gw:(g + 1) * gw]
        y_diag = []
        for pr in range(gw // LANES):
            mats = []
            for hh in range(LANES // SSM_HEAD_DIM):
                h = g * hpg + pr * (LANES // SSM_HEAD_DIM) + hh
                seg = cs_b[:, h * T:(h + 1) * T] - cs_t[h:h + 1, :]
                lmat = jnp.exp(jnp.where(causal, seg, -jnp.inf))
                mats.append((cb * lmat).astype(BF16))
            tile = g * (gw // LANES) + pr
            x2 = xb[:, tile * LANES:(tile + 1) * LANES]
            zero = jnp.zeros_like(x2)
            xbd = jnp.concatenate([jnp.where(lane2 < SSM_HEAD_DIM, x2, zero),
                                   jnp.where(lane2 < SSM_HEAD_DIM, zero, x2)], axis=0)
            y_diag.append(jnp.dot(jnp.concatenate(mats, axis=1), xbd, preferred_element_type=F32))
        y_parts.append(jnp.concatenate(y_diag, axis=1) + y_off)
        bg_t = xc[:, width + g * SSM_STATE:width + (g + 1) * SSM_STATE].T.astype(BF16)
        new_states = jnp.dot(bg_t, xdb[:, g * gw:(g + 1) * gw], preferred_element_type=F32)
        decay = ecs_x[T - 1:T, g * gw:(g + 1) * gw]
        st_ref[:, g * gw:(g + 1) * gw] = st_g * decay + new_states

    y = jnp.concatenate(y_parts, axis=1) + dskip_ref[...] * xs
    y = y * _silu(z_ref[...])
    outs = []
    for g in range(SSM_GROUPS):
        yg = y[:, g * gw:(g + 1) * gw]
        ms = jnp.mean(yg * yg, axis=-1, keepdims=True)
        outs.append(yg * lax.rsqrt(ms + NORM_EPS))
    y_ref[...] = (jnp.concatenate(outs, axis=1) * ng_ref[...]).astype(y_ref.dtype)

    @pl.when(c == nc - 1)
    def _():
        for j in range(ntile):
            hout_ref[j * LANES // SSM_HEAD_DIM:(j + 1) * LANES // SSM_HEAD_DIM] = (
                st_ref[:, j * LANES:(j + 1) * LANES].T.reshape(LANES // SSM_HEAD_DIM, SSM_HEAD_DIM, SSM_STATE))


def ssd_mixer(xbc, z, dtr, conv_prev, h0, conv_w, conv_b, dt_bias, a_log, d_skip, norm_g,
              batch, valid):
    mtot, conv_dim = xbc.shape
    width = z.shape[1]
    nheads = width // SSM_HEAD_DIM
    T = SSD_CHUNK
    nc = mtot // batch // T
    hid = jnp.arange(LANES)[:, None]
    e64 = (jnp.arange(width)[None, :] // SSM_HEAD_DIM == hid).astype(BF16)
    et = (jnp.arange(nheads * T)[None, :] // T == hid).astype(BF16)
    e64 = jnp.concatenate([e64] * 3, axis=0)
    et = jnp.concatenate([et] * 3, axis=0)
    pad_l = lambda v: jnp.pad(v.astype(F32), (0, LANES - nheads)).reshape(1, LANES)
    cw = jnp.pad(conv_w, ((0, SUBLANES - CONV_K), (0, 0)))
    full = lambda shape: pl.BlockSpec(shape, lambda b, c: (0,) * len(shape))
    y, hout = pl.pallas_call(
        functools.partial(_ssd_body, valid=valid, nheads=nheads),
        grid=(batch, nc),
        in_specs=[pl.BlockSpec((T, conv_dim), lambda b, c: (b * nc + c, 0)),
                  pl.BlockSpec((T, width), lambda b, c: (b * nc + c, 0)),
                  pl.BlockSpec((T, LANES), lambda b, c: (b * nc + c, 0)),
                  pl.BlockSpec((None, SUBLANES, conv_dim), lambda b, c: (b, 0, 0)),
                  pl.BlockSpec((None, nheads, SSM_HEAD_DIM, SSM_STATE), lambda b, c: (b, 0, 0, 0)),
                  full((SUBLANES, conv_dim)), full((1, conv_dim)), full((1, LANES)), full((1, LANES)),
                  full((1, width)), full((1, width)), full((3 * LANES, width)), full((3 * LANES, nheads * T))],
        out_specs=[pl.BlockSpec((T, width), lambda b, c: (b * nc + c, 0)),
                   pl.BlockSpec((None, nheads, SSM_HEAD_DIM, SSM_STATE), lambda b, c: (b, 0, 0, 0))],
        out_shape=[jax.ShapeDtypeStruct((mtot, width), BF16),
                   jax.ShapeDtypeStruct((batch, nheads, SSM_HEAD_DIM, SSM_STATE), F32)],
        scratch_shapes=[pltpu.VMEM((SUBLANES, conv_dim), F32),
                        pltpu.VMEM((SSM_STATE, width), F32),
                        pltpu.VMEM((SUBLANES + T, conv_dim), F32)],
        compiler_params=_cparams(("parallel", "arbitrary")),
        name="ssd_mixer",
    )(xbc, z, dtr, conv_prev, h0, cw, conv_b.reshape(1, conv_dim), pad_l(dt_bias), pad_l(a_log),
      jnp.repeat(d_skip.astype(F32), SSM_HEAD_DIM).reshape(1, width), norm_g.reshape(1, width), e64, et)
    return y, hout


def _layer(x_p, x_s, l, lam_init, dims, cache_k, cache_v, state_conv, state_ssm, page_table, w, kv_bufs,
           normed, next_norm_g):
    bp, seq, db, dseq = dims
    att_w, ssm_w, conv_dim, nheads = w["att_w"], w["ssm_w"], w["conv_dim"], w["nheads"]
    nh = att_w // HEAD_W
    offs = {"q": 0, "k": att_w, "v": 2 * att_w, "z": 3 * att_w, "xbc": 3 * att_w + ssm_w}
    ssm_args = (w["conv_w"][l], w["conv_b"][l], w["dt_bias"][l], w["a_log"][l], w["d_skip"][l],
                w["ssm_norm_g"][l])

    if normed is None:
        u_p = rmsnorm(x_p, w["attn_norm_g"][l], BF16)
        u_s = rmsnorm(x_s, w["attn_norm_g"][l], BF16)
        row_ss = None
    else:
        u_p, u_s, row_ss = normed[0], normed[1], normed[2:]
    mm = functools.partial(matmul, u_p, u_s, w["w_in_t"], l, w_t=True, row_ss=row_ss)
    q_p, q_s = mm(n_off=offs["q"], n=att_w, scale=D_HEAD ** -0.5 * math.log2(math.e))
    k_all, k_s = mm(n_off=offs["k"], n=att_w, stacked=True, out_buf=kv_bufs[0])
    v_all, v_s = mm(n_off=offs["v"], n=att_w, stacked=True, out_buf=kv_bufs[1])
    z_p, z_s = mm(n_off=offs["z"], n=ssm_w)
    xbc_p, xbc_s = mm(n_off=offs["xbc"], n=conv_dim)
    dtr_p, dtr_s = matmul(u_p, u_s, w["w_dt_t"], l, w_t=True, row_ss=row_ss)

    o_p = attn_prompt(q_p, k_all, v_all, l, w["lambda_qk"][l], w["subln_g"][l], lam_init, bp, seq)
    y_p, ssm_p = ssd_mixer(xbc_p, z_p, dtr_p, jnp.zeros((bp, SUBLANES, conv_dim), F32),
                           jnp.zeros((bp, nheads, SSM_HEAD_DIM, SSM_STATE), F32), *ssm_args,
                           batch=bp, valid=SSD_CHUNK)
    conv_p = xbc_p.reshape(bp, seq, conv_dim)[:, seq - (CONV_K - 1):]

    q5 = q_s.reshape(db, dseq, nh, 2, D_HEAD).transpose(0, 2, 1, 3, 4)
    sel = jnp.eye(2, dtype=F32)[None, None, :, None, :, None]
    qrows = (q5[:, :, None] * sel).reshape(db, nh * 8, HEAD_W)

    def tok_rows(t):
        t = t.reshape(db, dseq, nh, HEAD_W)
        return jnp.pad(t, ((0, 0), (0, SUBLANES - dseq), (0, 0), (0, 0))).reshape(db, SUBLANES * nh, HEAD_W)

    o_s = attn_sample(qrows, cache_k, cache_v, tok_rows(k_s), tok_rows(v_s), page_table,
                      w["lambda_qk"][l], w["subln_g"][l], lam_init, l, dseq)
    o_s = o_s.reshape(db, nh, 8, HEAD_W)[:, :, :dseq].transpose(0, 2, 1, 3).reshape(db * dseq, att_w)

    def pad_chunk(t):
        t = t.reshape(db, dseq, t.shape[-1])
        return jnp.pad(t, ((0, 0), (0, SSD_CHUNK - dseq), (0, 0))).reshape(db * SSD_CHUNK, t.shape[-1])

    cprev = jnp.pad(state_conv[l], ((0, 0), (SUBLANES - (CONV_K - 1), 0), (0, 0)))
    y_s, ssm_s = ssd_mixer(pad_chunk(xbc_s), pad_chunk(z_s), pad_chunk(dtr_s), cprev, state_ssm[l], *ssm_args,
                           batch=db, valid=dseq)
    y_s = y_s.reshape(db, SSD_CHUNK, ssm_w)[:, :dseq].reshape(db * dseq, ssm_w)
    conv_s = jnp.concatenate([state_conv[l], xbc_s.reshape(db, dseq, conv_dim)], axis=1)[:, -(CONV_K - 1):]

    h_p, h_s, hg_p, hg_s, hss_p, hss_s = matmul((o_p, y_p), (o_s.astype(BF16), y_s), w["w_out"], l,
                                                res=(x_p, x_s), norm_g=w["mlp_norm_g"][l])
    a_p, a_s = matmul(hg_p, hg_s, w["w_up"], l, out_dtype=BF16, act="relu2", row_ss=(hss_p, hss_s))
    x_p, x_s, *normed_next = matmul(a_p, a_s, w["w_down"], l, res=(h_p, h_s), norm_g=next_norm_g, tk=2048)

    outs = (conv_p, ssm_p, k_s.reshape(db, dseq, nh, HEAD_W), v_s.reshape(db, dseq, nh, HEAD_W), conv_s, ssm_s)
    return x_p, x_s, (k_all, v_all), outs, (tuple(normed_next) or None)


def kernel(x_prompt, x_sample, cache_k, cache_v, state_conv, state_ssm, page_table, attn_norm_g, w_in,
           lambda_qk, subln_g, conv_w, conv_b, dt_bias, a_log, d_skip, ssm_norm_g, w_out, mlp_norm_g,
           w_up, w_down, final_norm_g):
    bp, seq, d_model = x_prompt.shape
    db, dseq, _ = x_sample.shape
    depth = w_in.shape[0]
    nheads = dt_bias.shape[1]
    ssm_w = ssm_norm_g.shape[1]
    att_w = d_model - ssm_w
    nh = att_w // HEAD_W
    conv_dim = conv_w.shape[2]
    main_cols = 3 * att_w + ssm_w + conv_dim
    w_in_t = jnp.swapaxes(w_in, 1, 2)
    w = dict(att_w=att_w, ssm_w=ssm_w, conv_dim=conv_dim, nheads=nheads,
             w_in_t=w_in_t, w_dt_t=jnp.pad(w_in_t[:, main_cols:, :], ((0, 0), (0, LANES - nheads), (0, 0))),
             w_out=w_out, w_up=w_up, w_down=w_down,
             attn_norm_g=attn_norm_g, mlp_norm_g=mlp_norm_g, lambda_qk=lambda_qk, subln_g=subln_g,
             conv_w=conv_w, conv_b=conv_b, dt_bias=dt_bias, a_log=a_log, d_skip=d_skip, ssm_norm_g=ssm_norm_g)
    x_p = x_prompt.reshape(bp * seq, d_model)
    x_s = x_sample.reshape(db * dseq, d_model)
    kv_bufs = (jnp.zeros((depth, bp * seq, att_w), F32), jnp.zeros((depth, bp * seq, att_w), F32))
    per_layer, normed = [], None
    for l in range(depth):
        lam_init = 0.8 - 0.6 * math.exp(-0.3 * l)
        next_norm_g = attn_norm_g[l + 1] if l + 1 < depth else None
        x_p, x_s, kv_bufs, outs, normed = _layer(x_p, x_s, l, lam_init, (bp, seq, db, dseq), cache_k, cache_v,
                                                 state_conv, state_ssm, page_table, w, kv_bufs, normed,
                                                 next_norm_g)
        per_layer.append(outs)
    y_p = rmsnorm(x_p, final_norm_g, F32).reshape(bp, seq, d_model)
    y_s = rmsnorm(x_s, final_norm_g, F32).reshape(db, dseq, d_model)
    conv_p, ssm_p, k_s, v_s, conv_s, ssm_s = [jnp.stack([per_layer[l][i] for l in range(depth)])
                                              for i in range(6)]
    k_p = kv_bufs[0].reshape(depth, bp, seq, nh, HEAD_W)
    v_p = kv_bufs[1].reshape(depth, bp, seq, nh, HEAD_W)
    return (y_p, y_s, k_p, v_p, conv_p, ssm_p, k_s, v_s, conv_s, ssm_s)
```
